```python
import math
import jax, jax.numpy as jnp
from jax import lax
import numpy as np


D_MODEL = 2048
BATCH = 2
SEQ = 16384
DEPTH = 1
DEC_BATCH = 4
DEC_SEQ = 8192
PAST_LEN = 128

RW_HEAD = 64
RW_WIDTH = D_MODEL // 2
RW_HEADS = RW_WIDTH // RW_HEAD
DECAY_LORA = 96
ICLR_LORA = 96
GATE_LORA = 256
M_INNER = D_MODEL
M_HEADDIM = 64
M_HEADS = M_INNER // M_HEADDIM
M_STATE = 128
M_GROUPS = 8
CONV_W = 5
CHUNK = 128
N_MEM = 256
X_HEADS = 4
X_HEAD_DIM = D_MODEL // X_HEADS
D_FF = 4 * D_MODEL
ALPHA = (2.0 * DEPTH) ** 0.25
BETA = (8.0 * DEPTH) ** -0.25
LN_EPS = 1e-5
GN_EPS = 64e-5

RW_SHIFT_COLS = 3 * RW_WIDTH + DECAY_LORA + ICLR_LORA + GATE_LORA
M_CONV_CH = M_INNER + 2 * M_GROUPS * M_STATE
IN_COLS = RW_SHIFT_COLS + M_INNER + M_CONV_CH + M_HEADS + 2 * D_MODEL
RW_SPLITS = (RW_WIDTH, 2 * RW_WIDTH, 3 * RW_WIDTH, 3 * RW_WIDTH + DECAY_LORA, 3 * RW_WIDTH + DECAY_LORA + ICLR_LORA)
IN_SPLITS = (RW_SHIFT_COLS, RW_SHIFT_COLS + M_INNER, RW_SHIFT_COLS + M_INNER + M_CONV_CH, RW_SHIFT_COLS + M_INNER + M_CONV_CH + M_HEADS)

kernel_name = 'hybrid_rwkv7_mamba2_memory_encoder'


def layer_norm(x, g, b):
    xf = x.astype(jnp.float32)
    mu = jnp.mean(xf, axis=-1, keepdims=True)
    var = jnp.mean(jnp.square(xf - mu), axis=-1, keepdims=True)
    return ((xf - mu) * lax.rsqrt(var + LN_EPS) * g + b).astype(x.dtype)


def shift_prev(p):
    return jnp.pad(p[:, :-1], ((0, 0), (1, 0), (0, 0)))


def shift_next(p):
    return jnp.pad(p[:, 1:], ((0, 0), (0, 1), (0, 0)))


def bidir(fwd, bwd):
    return jnp.concatenate([fwd, jnp.flip(bwd, axis=1)], axis=0)


def merge_bidir(y, n):
    return y[:n] + jnp.flip(y[n:], axis=1)


def rw_heads(t):
    return t.reshape(t.shape[:-1] + (RW_HEADS, RW_HEAD))


def wkv7_scan(r, w, k, v, a, b):
    def step(S, inp):
        r_t, w_t, k_t, v_t, a_t, b_t = inp
        sa = jnp.einsum('nhij,nhj->nhi', S, a_t)
        S = S * w_t[:, :, None, :] + sa[..., None] * b_t[:, :, None, :] + v_t[..., None] * k_t[:, :, None, :]
        return S, jnp.einsum('nhij,nhj->nhi', S, r_t)
    n, h, d = r.shape[1:]
    _, y = lax.scan(step, jnp.zeros((n, h, d, d), jnp.float32), (r, w, k, v, a, b))
    return y


def rwkv7_branch(p, mu_prev, mu_next, w0, w2, a0, a2, g2, k_k, k_a, r_k, gn_g, gn_b):
    f32 = jnp.float32
    n = p.shape[0]
    T = p.shape[1]
    p = p + mu_prev * (shift_prev(p) - p) + mu_next * (shift_next(p) - p)
    r, k, v, dw, da, dg = jnp.split(p, RW_SPLITS, axis=-1)
    g = jax.nn.sigmoid(dg) @ g2
    hw = jnp.tanh(dw) @ w2
    ha = da @ a2
    logw = -jax.nn.softplus(-(w0[:, None, None, :] + hw[None]).astype(f32)) - 0.5
    decay = rw_heads(jnp.exp(-jnp.exp(logw)))
    a = jax.nn.sigmoid((a0[:, None, None, :] + ha[None]).astype(f32))
    kf = k.astype(f32)
    kk = rw_heads(kf * k_k)
    kk = kk / jnp.maximum(jnp.linalg.norm(kk, axis=-1, keepdims=True), 1e-12)
    k_dir = rw_heads(kf[None] * (1.0 + (a - 1.0) * k_a))
    a = rw_heads(a)
    rh = rw_heads(r.astype(f32))
    vh = rw_heads(v.astype(f32))
    xs = (bidir(rh, rh), bidir(decay[0], decay[1]), bidir(k_dir[0], k_dir[1]), bidir(vh, vh),
          bidir(-kk, -kk), bidir(kk * a[0], kk * a[1]))
    xs = tuple(jnp.moveaxis(t, 1, 0) for t in xs)
    y = merge_bidir(jnp.moveaxis(wkv7_scan(*xs), 0, 1), n)
    mu = jnp.mean(y, axis=-1, keepdims=True)
    var = jnp.mean(jnp.square(y - mu), axis=-1, keepdims=True)
    y = ((y - mu) * lax.rsqrt(var + GN_EPS)).reshape(n, T, RW_WIDTH) * gn_g + gn_b
    bonus = jnp.sum(rh * rw_heads(kf) * r_k, axis=-1, keepdims=True) * vh
    return ((y + bonus.reshape(n, T, RW_WIDTH)) * g).astype(p.dtype)


def depthwise_conv_centred(x, w, b):
    y = lax.conv_general_dilated(x, w[:, None, :], window_strides=(1,),
                                 padding=[(CONV_W // 2, CONV_W // 2)],
                                 dimension_numbers=('NWC', 'WIO', 'NWC'),
                                 feature_group_count=x.shape[-1])
    return y + b


def segsum(x):
    L = x.shape[-1]
    cs = jnp.cumsum(x, axis=-1)
    seg = cs[..., :, None] - cs[..., None, :]
    return jnp.where(jnp.tril(jnp.ones((L, L), bool)), seg, -jnp.inf)


def ssd_chunked(x, dA, Bm, Cm):
    n, T, H, P = x.shape
    c = T // CHUNK
    E = H // M_GROUPS
    x = x.reshape(n, c, CHUNK, M_GROUPS, E, P)
    Bm = Bm.reshape(n, c, CHUNK, M_GROUPS, M_STATE)
    Cm = Cm.reshape(n, c, CHUNK, M_GROUPS, M_STATE)
    dA = dA.reshape(n, c, CHUNK, M_GROUPS, E).transpose(0, 3, 4, 1, 2)
    cs = jnp.cumsum(dA, axis=-1)
    cb = jnp.einsum('nclgd,ncsgd->ngcls', Cm, Bm)
    m = cb[:, :, None] * jnp.exp(segsum(dA))
    y_diag = jnp.einsum('ngecls,ncsgep->nclgep', m, x)
    decay_states = jnp.exp(cs[..., -1:] - cs)
    states = jnp.einsum('nclgd,ngecl,nclgep->ncgepd', Bm, decay_states, x)
    chunk_decay = jnp.exp(cs[..., -1])

    def step(h, inp):
        s_c, dec_c = inp
        return h * dec_c[..., None, None] + s_c, h
    h0 = jnp.zeros((n, M_GROUPS, E, P, M_STATE), jnp.float32)
    _, h_in = lax.scan(step, h0, (jnp.moveaxis(states, 1, 0), jnp.moveaxis(chunk_decay, 3, 0)))
    h_in = jnp.moveaxis(h_in, 0, 1)
    y_off = jnp.einsum('nclgd,ncgepd,ngecl->nclgep', Cm, h_in, jnp.exp(cs))
    return (y_diag + y_off).reshape(n, T, H, P)


def mamba2_branch(z, xbc, dt_raw, conv_w, conv_b, dt_bias, a_log, d_skip, norm_g):
    f32 = jnp.float32
    n, T, _ = z.shape
    xbc = jax.nn.silu(depthwise_conv_centred(xbc, conv_w, conv_b))
    xs, Bm, Cm = jnp.split(xbc, (M_INNER, M_INNER + M_GROUPS * M_STATE), axis=-1)
    xh = xs.reshape(n, T, M_HEADS, M_HEADDIM)
    Bm = Bm.reshape(n, T, M_GROUPS, M_STATE).astype(f32)
    Cm = Cm.reshape(n, T, M_GROUPS, M_STATE).astype(f32)
    dt = jax.nn.softplus((dt_raw[None] + dt_bias[:, None, None, :]).astype(f32))
    A = -jnp.exp(a_log.astype(f32))
    dA = dt * A[:, None, None, :]
    xdt = xh[None].astype(f32) * dt[..., None]
    y = ssd_chunked(bidir(xdt[0], xdt[1]), bidir(dA[0], dA[1]), bidir(Bm, Bm), bidir(Cm, Cm))
    y = merge_bidir(y, n) + d_skip[:, None] * xh.astype(f32)
    y = y.reshape(n, T, M_INNER) * jax.nn.silu(z.astype(f32))
    yg = y.reshape(n, T, M_GROUPS, M_INNER // M_GROUPS)
    yg = yg * lax.rsqrt(jnp.mean(yg * yg, axis=-1, keepdims=True) + LN_EPS)
    return (yg.reshape(n, T, M_INNER) * norm_g).astype(z.dtype)


def memory_cross_attention(x, mem, w_q, w_kv, w_co):
    n, T, _ = x.shape
    nm = mem.shape[1]
    q = (x @ w_q).reshape(n, T, X_HEADS, X_HEAD_DIM)
    k, v = jnp.split(mem @ w_kv, 2, axis=-1)
    k = k.reshape(n, nm, X_HEADS, X_HEAD_DIM)
    v = v.reshape(n, nm, X_HEADS, X_HEAD_DIM)
    s = jnp.einsum('bthd,bmhd->bhtm', q, k).astype(jnp.float32) / math.sqrt(X_HEAD_DIM)
    pr = jax.nn.softmax(s, axis=-1).astype(v.dtype)
    o = jnp.einsum('bhtm,bmhd->bthd', pr, v).reshape(n, T, D_MODEL)
    return o @ w_co


def encoder_layer(x, mem, w_in, rw_mu_prev, rw_mu_next, rw_w0, rw_w2, rw_a0, rw_a2, rw_g2,
                  rw_k_k, rw_k_a, rw_r_k, rw_gn_g, rw_gn_b, m_conv_w, m_conv_b, m_dt_bias,
                  m_a_log, m_d, m_norm_g, w_br, w_bm, w_o, ln1_g, ln1_b, w_q, w_kv, w_co,
                  ln2_g, ln2_b, w_up, w_down, ln3_g, ln3_b):
    proj = x @ w_in
    p_rw, z, xbc, dt_raw, gates = jnp.split(proj, IN_SPLITS, axis=-1)
    u_rw = rwkv7_branch(p_rw, rw_mu_prev, rw_mu_next, rw_w0, rw_w2, rw_a0, rw_a2, rw_g2,
                        rw_k_k, rw_k_a, rw_r_k, rw_gn_g, rw_gn_b) @ w_br
    u_m = mamba2_branch(z, xbc, dt_raw, m_conv_w, m_conv_b, m_dt_bias, m_a_log, m_d, m_norm_g) @ w_bm
    g_rw, g_m = jnp.split(jax.nn.sigmoid(gates), 2, axis=-1)
    mix = (g_rw * u_rw + g_m * u_m) @ w_o
    x = layer_norm(ALPHA * x + mix, ln1_g, ln1_b)
    x = layer_norm(ALPHA * x + memory_cross_attention(x, mem, w_q, w_kv, w_co), ln2_g, ln2_b)
    h = jax.nn.relu(x @ w_up)
    return layer_norm(ALPHA * x + (h * h) @ w_down, ln3_g, ln3_b)


def encoder_trunk(x, mem, params):
    for l in range(DEPTH):
        x = encoder_layer(x, mem, *[p[l] for p in params])
    return x


def setup_inputs(seed: int = 0) -> dict:
    key = jax.random.key(seed)
    ks = iter(jax.random.split(key, 48))
    f32 = jnp.float32
    L = DEPTH

    def nrm(shape, scale):
        return jax.random.normal(next(ks), shape, f32) * scale

    def unif(shape, lo, hi):
        return jax.random.uniform(next(ks), shape, f32, lo, hi)

    chan = jnp.arange(RW_WIDTH, dtype=f32) / (RW_WIDTH - 1)
    w0_ramp = -6.0 + 5.0 * chan ** 0.85 + 0.5
    x_prompt = nrm((BATCH, SEQ, D_MODEL), 1.0)
    x_sample = nrm((DEC_BATCH, DEC_SEQ, D_MODEL), 1.0)
    mem_prompt = nrm((BATCH, N_MEM, D_MODEL), 1.0)
    mem_sample = nrm((DEC_BATCH, N_MEM, D_MODEL), 1.0)
    w_in = nrm((L, D_MODEL, IN_COLS), D_MODEL ** -0.5)
    rw_mu_prev = unif((L, RW_SHIFT_COLS), 0.0, 0.5)
    rw_mu_next = unif((L, RW_SHIFT_COLS), 0.0, 0.5)
    rw_w0 = w0_ramp + nrm((L, 2, RW_WIDTH), 0.1)
    rw_w2 = nrm((L, DECAY_LORA, RW_WIDTH), 0.5 * DECAY_LORA ** -0.5)
    rw_a0 = nrm((L, 2, RW_WIDTH), 0.1)
    rw_a2 = nrm((L, ICLR_LORA, RW_WIDTH), 0.5 * ICLR_LORA ** -0.5)
    rw_g2 = nrm((L, GATE_LORA, RW_WIDTH), GATE_LORA ** -0.5)
    rw_k_k = 0.85 + nrm((L, RW_WIDTH), 0.05)
    rw_k_a = 1.0 + nrm((L, RW_WIDTH), 0.05)
    rw_r_k = nrm((L, RW_HEADS, RW_HEAD), 0.1)
    rw_gn_g = 1.0 + nrm((L, RW_WIDTH), 0.05)
    rw_gn_b = nrm((L, RW_WIDTH), 0.02)
    m_conv_w = nrm((L, CONV_W, M_CONV_CH), CONV_W ** -0.5)
    m_conv_b = nrm((L, M_CONV_CH), 0.02)
    dt0 = jnp.exp(unif((L, 2, M_HEADS), math.log(1e-3), math.log(1e-1)))
    m_dt_bias = dt0 + jnp.log(-jnp.expm1(-dt0))
    m_a_log = jnp.log(unif((L, 2, M_HEADS), 1.0, 16.0))
    m_d = 1.0 + nrm((L, M_HEADS), 0.1)
    m_norm_g = 1.0 + nrm((L, M_INNER), 0.05)
    w_br = nrm((L, RW_WIDTH, D_MODEL), RW_WIDTH ** -0.5)
    w_bm = nrm((L, M_INNER, D_MODEL), M_INNER ** -0.5)
    w_o = nrm((L, D_MODEL, D_MODEL), BETA * D_MODEL ** -0.5)
    ln1_g = 1.0 + nrm((L, D_MODEL), 0.05)
    ln1_b = nrm((L, D_MODEL), 0.02)
    w_q = nrm((L, D_MODEL, D_MODEL), D_MODEL ** -0.5)
    w_kv = jnp.concatenate([nrm((L, D_MODEL, D_MODEL), D_MODEL ** -0.5),
                            nrm((L, D_MODEL, D_MODEL), BETA * D_MODEL ** -0.5)], axis=-1)
    w_co = nrm((L, D_MODEL, D_MODEL), BETA * D_MODEL ** -0.5)
    ln2_g = 1.0 + nrm((L, D_MODEL), 0.05)
    ln2_b = nrm((L, D_MODEL), 0.02)
    w_up = nrm((L, D_MODEL, D_FF), D_MODEL ** -0.5)
    w_down = nrm((L, D_FF, D_MODEL), BETA * D_FF ** -0.5)
    ln3_g = 1.0 + nrm((L, D_MODEL), 0.05)
    ln3_b = nrm((L, D_MODEL), 0.02)
    return {'x_prompt': x_prompt, 'x_sample': x_sample, 'mem_prompt': mem_prompt,
            'mem_sample': mem_sample, 'w_in': w_in, 'rw_mu_prev': rw_mu_prev,
            'rw_mu_next': rw_mu_next, 'rw_w0': rw_w0, 'rw_w2': rw_w2, 'rw_a0': rw_a0,
            'rw_a2': rw_a2, 'rw_g2': rw_g2, 'rw_k_k': rw_k_k, 'rw_k_a': rw_k_a,
            'rw_r_k': rw_r_k, 'rw_gn_g': rw_gn_g, 'rw_gn_b': rw_gn_b, 'm_conv_w': m_conv_w,
            'm_conv_b': m_conv_b, 'm_dt_bias': m_dt_bias, 'm_a_log': m_a_log, 'm_d': m_d,
            'm_norm_g': m_norm_g, 'w_br': w_br, 'w_bm': w_bm, 'w_o': w_o, 'ln1_g': ln1_g,
            'ln1_b': ln1_b, 'w_q': w_q, 'w_kv': w_kv, 'w_co': w_co, 'ln2_g': ln2_g,
            'ln2_b': ln2_b, 'w_up': w_up, 'w_down': w_down, 'ln3_g': ln3_g, 'ln3_b': ln3_b}


def reference(x_prompt, x_sample, mem_prompt, mem_sample, w_in, rw_mu_prev, rw_mu_next, rw_w0,
              rw_w2, rw_a0, rw_a2, rw_g2, rw_k_k, rw_k_a, rw_r_k, rw_gn_g, rw_gn_b, m_conv_w,
              m_conv_b, m_dt_bias, m_a_log, m_d, m_norm_g, w_br, w_bm, w_o, ln1_g, ln1_b, w_q,
              w_kv, w_co, ln2_g, ln2_b, w_up, w_down, ln3_g, ln3_b):
    params = (w_in, rw_mu_prev, rw_mu_next, rw_w0, rw_w2, rw_a0, rw_a2, rw_g2, rw_k_k, rw_k_a,
              rw_r_k, rw_gn_g, rw_gn_b, m_conv_w, m_conv_b, m_dt_bias, m_a_log, m_d, m_norm_g,
              w_br, w_bm, w_o, ln1_g, ln1_b, w_q, w_kv, w_co, ln2_g, ln2_b, w_up, w_down,
              ln3_g, ln3_b)
    y_prompt = encoder_trunk(x_prompt, mem_prompt, params)
    y_sample = encoder_trunk(x_sample, mem_sample, params)
    return (y_prompt, y_sample)
```

```python
import functools
import math

import jax
import jax.numpy as jnp
from jax import lax
from jax.experimental import pallas as pl
from jax.experimental.pallas import tpu as pltpu

F32 = jnp.float32
BF16 = jnp.bfloat16

D_MODEL = 2048
RW_HEAD = 64
RW_WIDTH = D_MODEL // 2
RW_HEADS = RW_WIDTH // RW_HEAD
DECAY_LORA = 96
ICLR_LORA = 96
GATE_LORA = 256
LORA_PAD = 128
M_INNER = D_MODEL
M_HEADDIM = 64
M_HEADS = M_INNER // M_HEADDIM
M_STATE = 128
M_GROUPS = 8
CONV_W = 5
N_MEM = 256
X_HEADS = 4
X_HEAD_DIM = D_MODEL // X_HEADS
D_FF = 4 * D_MODEL
DEPTH = 1
ALPHA = (2.0 * DEPTH) ** 0.25
LN_EPS = 1e-5
GN_EPS = 64e-5

RW_SHIFT_COLS = 3 * RW_WIDTH + DECAY_LORA + ICLR_LORA + GATE_LORA
RW_PAD_COLS = 3 * RW_WIDTH + 2 * LORA_PAD + GATE_LORA
M_CONV_CH = M_INNER + 2 * M_GROUPS * M_STATE

WKV_CHUNK = 64
SSD_CHUNK = 128
HALO = 16
LANES = 128
VMEM_LIMIT = 56 * 1024 * 1024


def _cparams(sem):
    return pltpu.CompilerParams(dimension_semantics=sem, vmem_limit_bytes=VMEM_LIMIT)


def _resident(shape):
    nd = len(shape)
    return pl.BlockSpec(shape, lambda *_: (0,) * nd, pipeline_mode=pl.Buffered(1))


def _split3(a):
    a1 = a.astype(BF16)
    r1 = a - a1.astype(F32)
    a2 = r1.astype(BF16)
    a3 = (r1 - a2.astype(F32)).astype(BF16)
    return a1, a2, a3


def _dot(a, b):
    return jnp.dot(a, b, preferred_element_type=F32)


def _dot_nt(a, b):
    return lax.dot_general(a, b, (((1,), (1,)), ((), ())), preferred_element_type=F32)


def _sel_left(m01, a):
    a1, a2, a3 = _split3(a)
    return _dot(m01, a1) + _dot(m01, a2) + _dot(m01, a3)


def _sel_right(a, m01, pieces=2):
    a1, a2, a3 = _split3(a)
    out = _dot(a1, m01) + _dot(a2, m01)
    if pieces == 3:
        out = out + _dot(a3, m01)
    return out


def _softplus(x):
    return jnp.maximum(x, 0.0) + jnp.log1p(jnp.exp(-jnp.abs(x)))


def _sigmoid(x):
    return 1.0 / (1.0 + jnp.exp(-x))


def _layer_norm(x, g, b):
    mu = jnp.mean(x, axis=-1, keepdims=True)
    xc = x - mu
    var = jnp.mean(xc * xc, axis=-1, keepdims=True)
    return xc * lax.rsqrt(var + LN_EPS) * g + b


def _mm_kernel(x_ref, w_ref, o_ref, *, act):
    acc = _dot(x_ref[...].astype(BF16), w_ref[...])
    if act == "sigmoid":
        acc = _sigmoid(acc)
    o_ref[...] = acc.astype(o_ref.dtype)


def _mm(x, w, *, act=None, out_dtype=BF16, tm=1024, tn=512):
    m, k = x.shape
    n = w.shape[1]
    tm = min(tm, m)
    tn = min(tn, n)
    return pl.pallas_call(
        functools.partial(_mm_kernel, act=act),
        grid=(m // tm, n // tn),
        in_specs=[pl.BlockSpec((tm, k), lambda i, j: (i, 0)),
                  pl.BlockSpec((k, tn), lambda i, j: (0, j))],
        out_specs=pl.BlockSpec((tm, tn), lambda i, j: (i, j)),
        out_shape=jax.ShapeDtypeStruct((m, n), out_dtype),
        compiler_params=_cparams(("parallel", "arbitrary")),
        name="matmul",
    )(x, w)


def _halo_specs(t, tt, width):
    per = tt // HALO
    last = t // HALO - 1

    def prev_map(b, i):
        return (b, jnp.maximum(i * per - 1, 0), 0)

    def next_map(b, i):
        return (b, jnp.minimum((i + 1) * per, last), 0)

    return (pl.BlockSpec((1, HALO, width), prev_map), pl.BlockSpec((1, HALO, width), next_map))


def _shifted(x, prev_rows, next_rows, off):
    tt = x.shape[0]
    rows = lax.broadcasted_iota(jnp.int32, (tt, 1), 0)
    if off < 0:
        out = pltpu.roll(x, -off, 0)
        for j in range(-off):
            out = jnp.where(rows == j, prev_rows[HALO + off + j:HALO + off + j + 1, :], out)
    else:
        out = pltpu.roll(x, tt - off, 0)
        for j in range(off):
            out = jnp.where(rows == tt - off + j, next_rows[j:j + 1, :], out)
    return out


def _rw_prep_kernel(p_ref, pp_ref, pn_ref, mup_ref, mun_ref, w2_ref, a2_ref, g2_ref, w0_ref, a0_ref,
                    kk_ref, rk_ref, hsum_ref,
                    r_out, k_out, v_out, kn_out, g_out, bonus_out, ld_out, a_out):
    i = pl.program_id(1)
    nt = pl.num_programs(1)
    p = p_ref[0].astype(F32)
    prev_rows = jnp.where(i > 0, pp_ref[0].astype(F32), 0.0)
    next_rows = jnp.where(i < nt - 1, pn_ref[0].astype(F32), 0.0)
    p_prev = _shifted(p, prev_rows, next_rows, -1)
    p_next = _shifted(p, prev_rows, next_rows, 1)
    ps = p + mup_ref[...] * (p_prev - p) + mun_ref[...] * (p_next - p)
    w = RW_WIDTH
    r = ps[:, 0:w]
    k = ps[:, w:2 * w]
    v = ps[:, 2 * w:3 * w]
    dw = ps[:, 3 * w:3 * w + LORA_PAD]
    da = ps[:, 3 * w + LORA_PAD:3 * w + 2 * LORA_PAD]
    dg = ps[:, 3 * w + 2 * LORA_PAD:]
    hw = _dot(jnp.tanh(dw).astype(BF16), w2_ref[...])
    ha = _dot(da.astype(BF16), a2_ref[...])
    g = _dot(_sigmoid(dg).astype(BF16), g2_ref[...])
    for d in range(2):
        logw = -_softplus(-(w0_ref[d:d + 1, :] + hw)) - 0.5
        ld_out[d, 0] = -jnp.exp(logw)
        a_out[d, 0] = _sigmoid(a0_ref[d:d + 1, :] + ha)
    hsum = hsum_ref[...]
    kkr = k * kk_ref[...]
    ss = _sel_right(kkr * kkr, hsum)
    kn = kkr * lax.rsqrt(jnp.maximum(ss, 1e-24))
    rk = _sel_right(r * k * rk_ref[...], hsum)
    r_out[0] = r.astype(r_out.dtype)
    k_out[0] = k.astype(k_out.dtype)
    v_out[0] = v.astype(v_out.dtype)
    kn_out[0] = kn.astype(kn_out.dtype)
    g_out[0] = g.astype(g_out.dtype)
    bonus_out[0] = (rk * v).astype(bonus_out.dtype)


def _rw_prep(p, mup, mun, w2, a2, g2, w0, a0, k_k, r_k, hsum, tt=256):
    n, t, c = p.shape
    tt = min(tt, t)
    w = RW_WIDTH
    tile = lambda width: pl.BlockSpec((1, tt, width), lambda b, i: (b, i, 0))
    dir_tile = pl.BlockSpec((2, 1, tt, w), lambda b, i: (0, b, i, 0))
    hp, hn = _halo_specs(t, tt, c)
    bf = jax.ShapeDtypeStruct((n, t, w), BF16)
    dd = jax.ShapeDtypeStruct((2, n, t, w), F32)
    return pl.pallas_call(
        _rw_prep_kernel,
        grid=(n, t // tt),
        in_specs=[tile(c), hp, hn, _resident(mup.shape), _resident(mun.shape), _resident(w2.shape),
                  _resident(a2.shape), _resident(g2.shape), _resident(w0.shape), _resident(a0.shape),
                  _resident(k_k.shape), _resident(r_k.shape), _resident(hsum.shape)],
        out_specs=[tile(w)] * 6 + [dir_tile, dir_tile],
        out_shape=[bf] * 6 + [dd, dd],
        compiler_params=_cparams(("parallel", "parallel")),
        name="rwkv_prep",
    )(p, p, p, mup, mun, w2, a2, g2, w0, a0, k_k, r_k, hsum)


def _inv_unit_lower(a):
    c = a.shape[0]
    eye = (lax.broadcasted_iota(jnp.int32, (c, c), 0) == lax.broadcasted_iota(jnp.int32, (c, c), 1))
    x = a + eye.astype(F32)
    pw = a
    steps = int(math.log2(c)) - 1
    for _ in range(steps):
        pb = pw.astype(BF16)
        pw = _dot(pb, pb)
        x = x + _dot(x.astype(BF16), pw.astype(BF16))
    return x


def _wkv_kernel(r_ref, k_ref, v_ref, kn_ref, ld_ref, a_ref, ka_ref, y_ref, h_ref):
    d = pl.program_id(1)
    c = pl.program_id(2)

    @pl.when(c == 0)
    def _():
        h_ref[...] = jnp.zeros_like(h_ref)

    cs = WKV_CHUNK
    sgn = 1 - 2 * d
    row = lax.broadcasted_iota(jnp.int32, (cs, 2 * cs), 0)
    col = lax.broadcasted_iota(jnp.int32, (cs, 2 * cs), 1) & (cs - 1)
    lag2 = (row - col) * sgn
    strict2 = lag2 > 0
    incl2 = lag2 >= 0
    m_incl = jnp.where(incl2[:, :cs], 1.0, 0.0).astype(BF16)

    ld = ld_ref[0, 0]
    cum = _sel_left(m_incl, ld)
    tot = jnp.sum(ld, axis=0, keepdims=True)
    e_cum = jnp.exp(cum)
    e_ncum = jnp.exp(-cum)
    e_excl = jnp.exp(cum - ld)
    e_rem = jnp.exp(tot - cum)
    wc = jnp.exp(tot)

    r = r_ref[0].astype(F32)
    k = k_ref[0].astype(F32)
    kn = kn_ref[0].astype(F32)
    a = a_ref[0, 0]
    v = v_ref[0]
    kdir = k * (1.0 + (a - 1.0) * ka_ref[...])
    b = kn * a
    rt = (r * e_cum).astype(BF16)
    at = (-kn * e_excl).astype(BF16)
    kt = (kdir * e_ncum).astype(BF16)
    bt = (b * e_ncum).astype(BF16)
    kh = (kdir * e_rem).astype(BF16)
    bh = (b * e_rem).astype(BF16)

    lane = lax.broadcasted_iota(jnp.int32, (1, LANES), 1)
    left = lane < RW_HEAD
    right = lane >= RW_HEAD
    blockdiag = (lax.broadcasted_iota(jnp.int32, (LANES, LANES), 0) // RW_HEAD
                 == lax.broadcasted_iota(jnp.int32, (LANES, LANES), 1) // RW_HEAD)
    zeros_cv = jnp.zeros((cs, LANES), BF16)

    for p in range(RW_HEADS // 2):
        sl = slice(p * LANES, (p + 1) * LANES)
        ar = jnp.concatenate([at[:, sl], rt[:, sl]], axis=0)
        bk = jnp.concatenate([bt[:, sl], kt[:, sl]], axis=0)
        bkh = jnp.concatenate([bh[:, sl], kh[:, sl]], axis=0)
        vp = v[:, sl]
        zv = jnp.concatenate([zeros_cv, vp], axis=0)
        ht = h_ref[p]
        ah = _dot_nt(ar, ht.astype(BF16))
        xs = []
        g_low = []
        for hh in range(2):
            arm = jnp.where(left if hh == 0 else right, ar, jnp.zeros_like(ar))
            gm = _dot_nt(arm, bk)
            g_up = jnp.where(strict2, gm[:cs], 0.0)
            g_low.append(jnp.where(incl2, gm[cs:], 0.0).astype(BF16))
            tinv = _inv_unit_lower(g_up[:, :cs])
            rhs = ah[:cs] + _dot(g_up.astype(BF16), zv)
            xs.append(_dot(tinv.astype(BF16), rhs.astype(BF16)))
        u = jnp.where(left, xs[0], xs[1])
        uv = jnp.concatenate([u, vp.astype(F32)], axis=0)
        uvb = uv.astype(BF16)
        ys = [ah[cs:] + _dot(g_low[hh], uvb) for hh in range(2)]
        y_ref[0, 0, :, sl] = jnp.where(left, ys[0], ys[1]).astype(y_ref.dtype)
        upd = _dot(uv.T.astype(BF16), bkh)
        h_ref[p] = ht * wc[:, sl] + jnp.where(blockdiag, upd, 0.0)


def _wkv(r, k, v, kn, ld, a, k_a):
    n, t, w = r.shape
    cs = WKV_CHUNK
    nc = t // cs

    def tmap(b, d, c):
        return (b, c + d * (nc - 1 - 2 * c), 0)

    def dmap(b, d, c):
        return (d, b, c + d * (nc - 1 - 2 * c), 0)

    tile = pl.BlockSpec((1, cs, w), tmap)
    dtile = pl.BlockSpec((1, 1, cs, w), dmap)
    return pl.pallas_call(
        _wkv_kernel,
        grid=(n, 2, nc),
        in_specs=[tile, tile, tile, tile, dtile, dtile, _resident(k_a.shape)],
        out_specs=dtile,
        out_shape=jax.ShapeDtypeStruct((2, n, t, w), F32),
        scratch_shapes=[pltpu.VMEM((RW_HEADS // 2, LANES, LANES), F32)],
        compiler_params=_cparams(("parallel", "arbitrary", "arbitrary")),
        name="wkv7_scan",
    )(r, k, v, kn, ld, a, k_a)


def _conv_kernel(x_ref, xp_ref, xn_ref, w_ref, b_ref, o_ref):
    i = pl.program_id(1)
    nt = pl.num_programs(1)
    x = x_ref[0].astype(F32)
    prev_rows = jnp.where(i > 0, xp_ref[0].astype(F32), 0.0)
    next_rows = jnp.where(i < nt - 1, xn_ref[0].astype(F32), 0.0)
    half = CONV_W // 2
    acc = x * w_ref[half:half + 1, :] + b_ref[...]
    for j in range(CONV_W):
        if j != half:
            acc = acc + _shifted(x, prev_rows, next_rows, j - half) * w_ref[j:j + 1, :]
    o_ref[0] = (acc * _sigmoid(acc)).astype(o_ref.dtype)


def _conv_silu(xbc, w, b, tt=256, tc=1024):
    n, t, c = xbc.shape
    tt = min(tt, t)
    per = tt // HALO
    nrow = t // HALO
    return pl.pallas_call(
        _conv_kernel,
        grid=(n, t // tt, c // tc),
        in_specs=[pl.BlockSpec((1, tt, tc), lambda b_, i, j: (b_, i, j)),
                  pl.BlockSpec((1, HALO, tc), lambda b_, i, j: (b_, jnp.maximum(i * per - 1, 0), j)),
                  pl.BlockSpec((1, HALO, tc), lambda b_, i, j: (b_, jnp.minimum((i + 1) * per, nrow - 1), j)),
                  pl.BlockSpec((CONV_W, tc), lambda b_, i, j: (0, j)),
                  pl.BlockSpec((1, tc), lambda b_, i, j: (0, j))],
        out_specs=pl.BlockSpec((1, tt, tc), lambda b_, i, j: (b_, i, j)),
        out_shape=jax.ShapeDtypeStruct((n, t, c), BF16),
        compiler_params=_cparams(("parallel", "parallel", "parallel")),
        name="mamba_conv",
    )(xbc, xbc, xbc, w, b)


def _ssd_kernel(xbc_ref, dt_ref, dtb_ref, alog_ref, hexp_ref, y_ref, h_ref):
    d = pl.program_id(1)
    c = pl.program_id(2)

    @pl.when(c == 0)
    def _():
        h_ref[...] = jnp.zeros_like(h_ref)

    ln = SSD_CHUNK
    row = lax.broadcasted_iota(jnp.int32, (ln, ln), 0)
    col = lax.broadcasted_iota(jnp.int32, (ln, ln), 1)
    incl = (row - col) * (1 - 2 * d) >= 0
    m_incl = jnp.where(incl, 1.0, 0.0).astype(BF16)

    dt = _softplus(dt_ref[0] + dtb_ref[0])
    da = dt * (-jnp.exp(alog_ref[0]))
    cum = _sel_left(m_incl, da)
    tot = jnp.sum(da, axis=0, keepdims=True)
    cum_t = cum.T
    e_cum = jnp.exp(cum)
    dst = jnp.exp(tot - cum) * dt
    cdec = jnp.exp(jnp.broadcast_to(tot, (8, LANES)))
    cdec_x = _sel_right(cdec, hexp_ref[...], pieces=3)[0:1]

    lane = lax.broadcasted_iota(jnp.int32, (1, LANES), 1)
    left = lane < M_HEADDIM
    gw = M_INNER // M_GROUPS
    heads_per_group = M_HEADS // M_GROUPS
    for g in range(M_GROUPS):
        bg = xbc_ref[0, :, M_INNER + g * M_STATE:M_INNER + (g + 1) * M_STATE]
        cg = xbc_ref[0, :, M_INNER + M_GROUPS * M_STATE + g * M_STATE:
                     M_INNER + M_GROUPS * M_STATE + (g + 1) * M_STATE]
        cgf = cg.astype(F32)
        cb = _dot_nt(cg, bg)
        hg = h_ref[g]
        hgb = hg.astype(BF16)
        xparts = []
        for pr in range(heads_per_group // 2):
            lo = g * gw + pr * LANES
            xp = xbc_ref[0, :, lo:lo + LANES].astype(F32)
            e0 = g * heads_per_group + 2 * pr
            dt_x = jnp.where(left, dt[:, e0:e0 + 1], dt[:, e0 + 1:e0 + 2])
            xdt = (xp * dt_x).astype(BF16)
            ys = []
            for hh in range(2):
                e = e0 + hh
                seg = cum[:, e:e + 1] - cum_t[e:e + 1, :]
                decay = jnp.exp(jnp.where(incl, seg, -1e30))
                lhs = jnp.concatenate([(cb * decay).astype(BF16),
                                       (cgf * e_cum[:, e:e + 1]).astype(BF16)], axis=1)
                rhs = jnp.concatenate([xdt, hgb[:, pr * LANES:(pr + 1) * LANES]], axis=0)
                ys.append(_dot(lhs, rhs))
            y_ref[0, 0, :, lo:lo + LANES] = jnp.where(left, ys[0], ys[1]).astype(y_ref.dtype)
            dst_x = jnp.where(left, dst[:, e0:e0 + 1], dst[:, e0 + 1:e0 + 2])
            xparts.append((xp * dst_x).astype(BF16))
        xg = jnp.concatenate(xparts, axis=1)
        upd = _dot(bg.astype(F32).T.astype(BF16), xg)
        h_ref[g] = hg * cdec_x[:, g * gw:(g + 1) * gw] + upd


def _ssd(xbc_act, dt_raw, dt_bias, a_log, hexp):
    n, t, c = xbc_act.shape
    ln = SSD_CHUNK
    nc = t // ln

    def tmap(b, d, ci):
        return (b, ci + d * (nc - 1 - 2 * ci), 0)

    def dmap(b, d, ci):
        return (d, b, ci + d * (nc - 1 - 2 * ci), 0)

    par = pl.BlockSpec((1, 1, LANES), lambda b, d, ci: (d, 0, 0))
    return pl.pallas_call(
        _ssd_kernel,
        grid=(n, 2, nc),
        in_specs=[pl.BlockSpec((1, ln, c), tmap), pl.BlockSpec((1, ln, LANES), tmap), par, par,
                  _resident(hexp.shape)],
        out_specs=pl.BlockSpec((1, 1, ln, M_INNER), dmap),
        out_shape=jax.ShapeDtypeStruct((2, n, t, M_INNER), F32),
        scratch_shapes=[pltpu.VMEM((M_GROUPS, M_STATE, M_INNER // M_GROUPS), F32)],
        compiler_params=_cparams(("parallel", "arbitrary", "arbitrary")),
        name="ssd_scan",
    )(xbc_act, dt_raw, dt_bias, a_log, hexp)


def _merge_kernel(yw_ref, bonus_ref, g_ref, ym_ref, xs_ref, z_ref, gates_ref, x_ref,
                  hsum_ref, gng_ref, gnb_ref, dskip_ref, ng_ref, wbr_ref, wbm_ref, wo_ref,
                  lng_ref, lnb_ref, o_ref):
    y = yw_ref[0, 0] + yw_ref[1, 0]
    hsum = hsum_ref[...]
    mu = _sel_right(y, hsum) * (1.0 / RW_HEAD)
    yc = y - mu
    var = _sel_right(yc * yc, hsum) * (1.0 / RW_HEAD)
    yn = yc * lax.rsqrt(var + GN_EPS) * gng_ref[...] + gnb_ref[...]
    rw = (yn + bonus_ref[0].astype(F32)) * g_ref[0].astype(F32)
    z = z_ref[0].astype(F32)
    ym = ym_ref[0, 0] + ym_ref[1, 0] + dskip_ref[...] * xs_ref[0].astype(F32)
    ym = ym * (z * _sigmoid(z))
    gw = M_INNER // M_GROUPS
    parts = []
    for gi in range(M_GROUPS):
        yg = ym[:, gi * gw:(gi + 1) * gw]
        ms = jnp.mean(yg * yg, axis=-1, keepdims=True)
        parts.append(yg * lax.rsqrt(ms + LN_EPS))
    mo = jnp.concatenate(parts, axis=1) * ng_ref[...]
    u_rw = _dot(rw.astype(BF16), wbr_ref[...])
    u_m = _dot(mo.astype(BF16), wbm_ref[...])
    gates = gates_ref[0]
    mixed = gates[:, :D_MODEL].astype(F32) * u_rw + gates[:, D_MODEL:].astype(F32) * u_m
    mix = _dot(mixed.astype(BF16), wo_ref[...])
    o_ref[0] = _layer_norm(ALPHA * x_ref[0] + mix, lng_ref[...], lnb_ref[...])


def _merge(yw, bonus, g, ym, xbc_act, z, gates, x, hsum, gn_g, gn_b, dskip, norm_g, w_br, w_bm, w_o,
           ln_g, ln_b, tm=256):
    n, t, _ = x.shape
    tm = min(tm, t)
    tile = lambda width: pl.BlockSpec((1, tm, width), lambda b, i: (b, i, 0))
    dtile = lambda width: pl.BlockSpec((2, 1, tm, width), lambda b, i: (0, b, i, 0))
    res = [hsum, gn_g, gn_b, dskip, norm_g, w_br, w_bm, w_o, ln_g, ln_b]
    return pl.pallas_call(
        _merge_kernel,
        grid=(n, t // tm),
        in_specs=[dtile(RW_WIDTH), tile(RW_WIDTH), tile(RW_WIDTH), dtile(M_INNER), tile(M_INNER),
                  tile(M_INNER), tile(2 * D_MODEL), tile(D_MODEL)] + [_resident(a.shape) for a in res],
        out_specs=tile(D_MODEL),
        out_shape=jax.ShapeDtypeStruct(x.shape, F32),
        compiler_params=_cparams(("parallel", "parallel")),
        name="merge_ln1",
    )(yw, bonus, g, ym, xbc_act, z, gates, x, *res)


def _attn_kernel(x_ref, kv_ref, wq_ref, wco_ref, lng_ref, lnb_ref, o_ref):
    x = x_ref[0]
    q = _dot(x.astype(BF16), wq_ref[...]).astype(BF16)
    scale = 1.0 / math.sqrt(X_HEAD_DIM)
    out = None
    for h in range(X_HEADS):
        sl = slice(h * X_HEAD_DIM, (h + 1) * X_HEAD_DIM)
        s = _dot_nt(q[:, sl], kv_ref[0, :, sl]) * scale
        s = s - jnp.max(s, axis=-1, keepdims=True)
        e = jnp.exp(s)
        pr = e / jnp.sum(e, axis=-1, keepdims=True)
        o = _dot(pr.astype(BF16), kv_ref[0, :, D_MODEL + h * X_HEAD_DIM:D_MODEL + (h + 1) * X_HEAD_DIM])
        c = _dot(o.astype(BF16), wco_ref[sl, :])
        out = c if out is None else out + c
    o_ref[0] = _layer_norm(ALPHA * x + out, lng_ref[...], lnb_ref[...])


def _attention(x, kv, w_q, w_co, ln_g, ln_b, tm=512):
    n, t, _ = x.shape
    tm = min(tm, t)
    tile = pl.BlockSpec((1, tm, D_MODEL), lambda b, i: (b, i, 0))
    return pl.pallas_call(
        _attn_kernel,
        grid=(n, t // tm),
        in_specs=[tile, pl.BlockSpec((1, N_MEM, 2 * D_MODEL), lambda b, i: (b, 0, 0)),
                  _resident(w_q.shape), _resident(w_co.shape), _resident(ln_g.shape), _resident(ln_b.shape)],
        out_specs=tile,
        out_shape=jax.ShapeDtypeStruct(x.shape, F32),
        compiler_params=_cparams(("parallel", "parallel")),
        name="mem_attention_ln2",
    )(x, kv, w_q, w_co, ln_g, ln_b)


def _mlp_kernel(x_ref, wu_ref, wd_ref, lng_ref, lnb_ref, o_ref, acc_ref, xb_ref):
    f = pl.program_id(1)

    @pl.when(f == 0)
    def _():
        xb_ref[...] = x_ref[...].astype(BF16)
        acc_ref[...] = jnp.zeros_like(acc_ref)

    h = jnp.maximum(_dot(xb_ref[...], wu_ref[...]), 0.0)
    acc_ref[...] += _dot((h * h).astype(BF16), wd_ref[...])

    @pl.when(f == pl.num_programs(1) - 1)
    def _():
        o_ref[...] = _layer_norm(ALPHA * x_ref[...] + acc_ref[...], lng_ref[...], lnb_ref[...])


def _mlp(x, w_up, w_down, ln_g, ln_b, tm=512, tf=1024):
    m, dm = x.shape
    tm = min(tm, m)
    return pl.pallas_call(
        _mlp_kernel,
        grid=(m // tm, D_FF // tf),
        in_specs=[pl.BlockSpec((tm, dm), lambda i, f: (i, 0)),
                  pl.BlockSpec((dm, tf), lambda i, f: (0, f)),
                  pl.BlockSpec((tf, dm), lambda i, f: (f, 0)),
                  _resident(ln_g.shape), _resident(ln_b.shape)],
        out_specs=pl.BlockSpec((tm, dm), lambda i, f: (i, 0)),
        out_shape=jax.ShapeDtypeStruct((m, dm), F32),
        scratch_shapes=[pltpu.VMEM((tm, dm), F32), pltpu.VMEM((tm, dm), BF16)],
        compiler_params=_cparams(("parallel", "arbitrary")),
        name="mlp_ln3",
    )(x, w_up, w_down, ln_g, ln_b)


def _pad_cols(a, width):
    return jnp.pad(a, [(0, 0)] * (a.ndim - 1) + [(0, width - a.shape[-1])])


def _rw_cols(a):
    w3 = 3 * RW_WIDTH
    return jnp.concatenate([a[..., :w3],
                            _pad_cols(a[..., w3:w3 + DECAY_LORA], LORA_PAD),
                            _pad_cols(a[..., w3 + DECAY_LORA:w3 + DECAY_LORA + ICLR_LORA], LORA_PAD),
                            a[..., w3 + DECAY_LORA + ICLR_LORA:RW_SHIFT_COLS]], axis=-1)


def _prepare(w_in, rw_mu_prev, rw_mu_next, rw_w0, rw_w2, rw_a0, rw_a2, rw_g2, rw_k_k, rw_k_a, rw_r_k,
             rw_gn_g, rw_gn_b, m_conv_w, m_conv_b, m_dt_bias, m_a_log, m_d, m_norm_g, w_br, w_bm, w_o,
             ln1_g, ln1_b, w_q, w_kv, w_co, ln2_g, ln2_b, w_up, w_down, ln3_g, ln3_b):
    l = 0
    w = w_in[l]
    c0 = RW_SHIFT_COLS
    c1 = c0 + M_INNER
    c2 = c1 + M_CONV_CH
    c3 = c2 + M_HEADS
    row = lambda a: a.reshape(1, -1).astype(F32)
    head = jnp.arange(RW_WIDTH) // RW_HEAD
    lane_head = jnp.arange(M_INNER) // M_HEADDIM
    return dict(
        w_rw=_rw_cols(w[:, :c0]).astype(BF16),
        w_z=w[:, c0:c1].astype(BF16),
        w_xbc=w[:, c1:c2].astype(BF16),
        w_dt=_pad_cols(w[:, c2:c3], LANES).astype(BF16),
        w_gates=w[:, c3:].astype(BF16),
        mup=row(_rw_cols(rw_mu_prev[l])), mun=row(_rw_cols(rw_mu_next[l])),
        w0=rw_w0[l].astype(F32), a0=rw_a0[l].astype(F32),
        w2=jnp.pad(rw_w2[l], ((0, LORA_PAD - DECAY_LORA), (0, 0))).astype(BF16),
        a2=jnp.pad(rw_a2[l], ((0, LORA_PAD - ICLR_LORA), (0, 0))).astype(BF16),
        g2=rw_g2[l].astype(BF16),
        k_k=row(rw_k_k[l]), k_a=row(rw_k_a[l]), r_k=row(rw_r_k[l]),
        gn_g=row(rw_gn_g[l]), gn_b=row(rw_gn_b[l]),
        hsum=(head[:, None] == head[None, :]).astype(BF16),
        conv_w=m_conv_w[l].astype(F32), conv_b=row(m_conv_b[l]),
        dt_bias=_pad_cols(m_dt_bias[l], LANES).reshape(2, 1, LANES).astype(F32),
        a_log=_pad_cols(m_a_log[l], LANES).reshape(2, 1, LANES).astype(F32),
        hexp=(jnp.arange(LANES)[:, None] == lane_head[None, :]).astype(BF16),
        dskip=row(jnp.repeat(m_d[l], M_HEADDIM)), norm_g=row(m_norm_g[l]),
        w_br=w_br[l].astype(BF16), w_bm=w_bm[l].astype(BF16), w_o=w_o[l].astype(BF16),
        ln1_g=row(ln1_g[l]), ln1_b=row(ln1_b[l]),
        w_q=w_q[l].astype(BF16), w_kv=w_kv[l].astype(BF16), w_co=w_co[l].astype(BF16),
        ln2_g=row(ln2_g[l]), ln2_b=row(ln2_b[l]),
        w_up=w_up[l].astype(BF16), w_down=w_down[l].astype(BF16),
        ln3_g=row(ln3_g[l]), ln3_b=row(ln3_b[l]),
    )


def _encoder_layer(x, mem, p):
    n, t, dm = x.shape
    x2 = x.reshape(n * t, dm)
    as3 = lambda a: a.reshape(n, t, a.shape[-1])
    p_rw = as3(_mm(x2, p["w_rw"]))
    z = as3(_mm(x2, p["w_z"]))
    xbc = as3(_mm(x2, p["w_xbc"]))
    dt_raw = as3(_mm(x2, p["w_dt"], out_dtype=F32))
    gates = as3(_mm(x2, p["w_gates"], act="sigmoid"))

    r, k, v, kn, g, bonus, ld, a = _rw_prep(p_rw, p["mup"], p["mun"], p["w2"], p["a2"], p["g2"],
                                            p["w0"], p["a0"], p["k_k"], p["r_k"], p["hsum"])
    yw = _wkv(r, k, v, kn, ld, a, p["k_a"])

    xbc_act = _conv_silu(xbc, p["conv_w"], p["conv_b"])
    ym = _ssd(xbc_act, dt_raw, p["dt_bias"], p["a_log"], p["hexp"])

    x1 = _merge(yw, bonus, g, ym, xbc_act, z, gates, x, p["hsum"], p["gn_g"], p["gn_b"], p["dskip"],
                p["norm_g"], p["w_br"], p["w_bm"], p["w_o"], p["ln1_g"], p["ln1_b"])

    kv = _mm(mem.reshape(n * N_MEM, dm), p["w_kv"]).reshape(n, N_MEM, 2 * dm)
    x2_ = _attention(x1, kv, p["w_q"], p["w_co"], p["ln2_g"], p["ln2_b"])
    out = _mlp(x2_.reshape(n * t, dm), p["w_up"], p["w_down"], p["ln3_g"], p["ln3_b"])
    return out.reshape(n, t, dm)


def kernel(x_prompt, x_sample, mem_prompt, mem_sample, w_in, rw_mu_prev, rw_mu_next, rw_w0, rw_w2, rw_a0, rw_a2, rw_g2, rw_k_k, rw_k_a, rw_r_k, rw_gn_g, rw_gn_b, m_conv_w, m_conv_b, m_dt_bias, m_a_log, m_d, m_norm_g, w_br, w_bm, w_o, ln1_g, ln1_b, w_q, w_kv, w_co, ln2_g, ln2_b, w_up, w_down, ln3_g, ln3_b):
    p = _prepare(w_in, rw_mu_prev, rw_mu_next, rw_w0, rw_w2, rw_a0, rw_a2, rw_g2, rw_k_k, rw_k_a, rw_r_k,
                 rw_gn_g, rw_gn_b, m_conv_w, m_conv_b, m_dt_bias, m_a_log, m_d, m_norm_g, w_br, w_bm, w_o,
                 ln1_g, ln1_b, w_q, w_kv, w_co, ln2_g, ln2_b, w_up, w_down, ln3_g, ln3_b)
    return (_encoder_layer(x_prompt, mem_prompt, p), _encoder_layer(x_sample, mem_sample, p))
```

```python
import functools
import math

import jax
import jax.numpy as jnp
from jax import lax
from jax.experimental import pallas as pl
from jax.experimental.pallas import tpu as pltpu

F32 = jnp.float32
BF16 = jnp.bfloat16

D_MODEL = 2048
RW_HEAD = 64
RW_WIDTH = D_MODEL // 2
RW_HEADS = RW_WIDTH // RW_HEAD
DECAY_LORA = 96
ICLR_LORA = 96
GATE_LORA = 256
LORA_PAD = 128
M_INNER = D_MODEL
M_HEADDIM = 64
M_HEADS = M_INNER // M_HEADDIM
M_STATE = 128
M_GROUPS = 8
CONV_W = 5
N_MEM = 256
X_HEADS = 4
X_HEAD_DIM = D_MODEL // X_HEADS
D_FF = 4 * D_MODEL
DEPTH = 1
ALPHA = (2.0 * DEPTH) ** 0.25
LN_EPS = 1e-5
GN_EPS = 64e-5

RW_SHIFT_COLS = 3 * RW_WIDTH + DECAY_LORA + ICLR_LORA + GATE_LORA
RW_PAD_COLS = 3 * RW_WIDTH + 2 * LORA_PAD + GATE_LORA
M_CONV_CH = M_INNER + 2 * M_GROUPS * M_STATE

WKV_CHUNK = 64
SSD_CHUNK = 128
HALO = 16
LANES = 128
VMEM_LIMIT = 56 * 1024 * 1024


def _cparams(sem):
    return pltpu.CompilerParams(dimension_semantics=sem, vmem_limit_bytes=VMEM_LIMIT)


def _resident(shape):
    nd = len(shape)
    return pl.BlockSpec(shape, lambda *_: (0,) * nd, pipeline_mode=pl.Buffered(1))


def _split3(a):
    a1 = a.astype(BF16)
    r1 = a - a1.astype(F32)
    a2 = r1.astype(BF16)
    a3 = (r1 - a2.astype(F32)).astype(BF16)
    return a1, a2, a3


def _dot(a, b):
    return jnp.dot(a, b, preferred_element_type=F32)


def _dot_nt(a, b):
    return lax.dot_general(a, b, (((1,), (1,)), ((), ())), preferred_element_type=F32)


def _sel_left(m01, a):
    a1, a2, a3 = _split3(a)
    return _dot(m01, a1) + _dot(m01, a2) + _dot(m01, a3)


def _sel_right(a, m01, pieces=2):
    a1, a2, a3 = _split3(a)
    out = _dot(a1, m01) + _dot(a2, m01)
    if pieces == 3:
        out = out + _dot(a3, m01)
    return out


def _head_sum(a, down, up):
    return _sel_right(_sel_right(a, down), up)


def _softplus(x):
    return jnp.maximum(x, 0.0) + jnp.log1p(jnp.exp(-jnp.abs(x)))


def _sigmoid(x):
    return 1.0 / (1.0 + jnp.exp(-x))


def _layer_norm(x, g, b):
    mu = jnp.mean(x, axis=-1, keepdims=True)
    xc = x - mu
    var = jnp.mean(xc * xc, axis=-1, keepdims=True)
    return xc * lax.rsqrt(var + LN_EPS) * g + b


def _mm_kernel(x_ref, w_ref, o_ref, xb_ref, *, act):
    @pl.when(pl.program_id(1) == 0)
    def _():
        xb_ref[...] = x_ref[...].astype(BF16)

    acc = _dot(xb_ref[...], w_ref[...])
    if act == "sigmoid":
        acc = _sigmoid(acc)
    o_ref[...] = acc.astype(o_ref.dtype)


def _mm(x, w, *, act=None, out_dtype=BF16, tm=1024, tn=512):
    m, k = x.shape
    n = w.shape[1]
    tm = min(tm, m)
    tn = min(tn, n)
    return pl.pallas_call(
        functools.partial(_mm_kernel, act=act),
        grid=(m // tm, n // tn),
        in_specs=[pl.BlockSpec((tm, k), lambda i, j: (i, 0)),
                  pl.BlockSpec((k, tn), lambda i, j: (0, j))],
        out_specs=pl.BlockSpec((tm, tn), lambda i, j: (i, j)),
        out_shape=jax.ShapeDtypeStruct((m, n), out_dtype),
        scratch_shapes=[pltpu.VMEM((tm, k), BF16)],
        compiler_params=_cparams(("parallel", "arbitrary")),
        name="matmul",
    )(x, w)


def _halo_specs(t, tt, width):
    per = tt // HALO
    last = t // HALO - 1

    def prev_map(b, i):
        return (b, jnp.maximum(i * per - 1, 0), 0)

    def next_map(b, i):
        return (b, jnp.minimum((i + 1) * per, last), 0)

    return (pl.BlockSpec((1, HALO, width), prev_map), pl.BlockSpec((1, HALO, width), next_map))


def _shifted(x, prev_rows, next_rows, off):
    tt = x.shape[0]
    rows = lax.broadcasted_iota(jnp.int32, (tt, 1), 0)
    if off < 0:
        out = pltpu.roll(x, -off, 0)
        for j in range(-off):
            out = jnp.where(rows == j, prev_rows[HALO + off + j:HALO + off + j + 1, :], out)
    else:
        out = pltpu.roll(x, tt - off, 0)
        for j in range(off):
            out = jnp.where(rows == tt - off + j, next_rows[j:j + 1, :], out)
    return out


def _rw_prep_kernel(p_ref, pp_ref, pn_ref, mup_ref, mun_ref, w2_ref, a2_ref, g2_ref, w0_ref, a0_ref,
                    kk_ref, rk_ref, hdown_ref, hup_ref,
                    r_out, k_out, v_out, kn_out, g_out, bonus_out, ld_out, a_out):
    i = pl.program_id(1)
    nt = pl.num_programs(1)
    p = p_ref[0].astype(F32)
    prev_rows = jnp.where(i > 0, pp_ref[0].astype(F32), 0.0)
    next_rows = jnp.where(i < nt - 1, pn_ref[0].astype(F32), 0.0)
    p_prev = _shifted(p, prev_rows, next_rows, -1)
    p_next = _shifted(p, prev_rows, next_rows, 1)
    ps = p + mup_ref[...] * (p_prev - p) + mun_ref[...] * (p_next - p)
    w = RW_WIDTH
    r = ps[:, 0:w]
    k = ps[:, w:2 * w]
    v = ps[:, 2 * w:3 * w]
    dw = ps[:, 3 * w:3 * w + LORA_PAD]
    da = ps[:, 3 * w + LORA_PAD:3 * w + 2 * LORA_PAD]
    dg = ps[:, 3 * w + 2 * LORA_PAD:]
    hw = _dot(jnp.tanh(dw).astype(BF16), w2_ref[...])
    ha = _dot(da.astype(BF16), a2_ref[...])
    g = _dot(_sigmoid(dg).astype(BF16), g2_ref[...])
    for d in range(2):
        logw = -_softplus(-(w0_ref[d:d + 1, :] + hw)) - 0.5
        ld_out[d, 0] = -jnp.exp(logw)
        a_out[d, 0] = _sigmoid(a0_ref[d:d + 1, :] + ha)
    hdown = hdown_ref[...]
    hup = hup_ref[...]
    kkr = k * kk_ref[...]
    ss = _head_sum(kkr * kkr, hdown, hup)
    kn = kkr * lax.rsqrt(jnp.maximum(ss, 1e-24))
    rk = _head_sum(r * k * rk_ref[...], hdown, hup)
    r_out[0] = r.astype(r_out.dtype)
    k_out[0] = k.astype(k_out.dtype)
    v_out[0] = v.astype(v_out.dtype)
    kn_out[0] = kn.astype(kn_out.dtype)
    g_out[0] = g.astype(g_out.dtype)
    bonus_out[0] = (rk * v).astype(bonus_out.dtype)


def _rw_prep(p, mup, mun, w2, a2, g2, w0, a0, k_k, r_k, hdown, hup, tt=256):
    n, t, c = p.shape
    tt = min(tt, t)
    w = RW_WIDTH
    tile = lambda width: pl.BlockSpec((1, tt, width), lambda b, i: (b, i, 0))
    dir_tile = pl.BlockSpec((2, 1, tt, w), lambda b, i: (0, b, i, 0))
    hp, hn = _halo_specs(t, tt, c)
    bf = jax.ShapeDtypeStruct((n, t, w), BF16)
    dd = jax.ShapeDtypeStruct((2, n, t, w), F32)
    return pl.pallas_call(
        _rw_prep_kernel,
        grid=(n, t // tt),
        in_specs=[tile(c), hp, hn, _resident(mup.shape), _resident(mun.shape), _resident(w2.shape),
                  _resident(a2.shape), _resident(g2.shape), _resident(w0.shape), _resident(a0.shape),
                  _resident(k_k.shape), _resident(r_k.shape), _resident(hdown.shape), _resident(hup.shape)],
        out_specs=[tile(w)] * 6 + [dir_tile, dir_tile],
        out_shape=[bf] * 6 + [dd, dd],
        compiler_params=_cparams(("parallel", "parallel")),
        name="rwkv_prep",
    )(p, p, p, mup, mun, w2, a2, g2, w0, a0, k_k, r_k, hdown, hup)


def _wkv_kernel(r_ref, k_ref, v_ref, kn_ref, ld_ref, a_ref, ka_ref, y_ref, h_ref, *, nchunk):
    d = pl.program_id(1)
    c = pl.program_id(2)

    @pl.when(c == 0)
    def _():
        h_ref[...] = jnp.zeros_like(h_ref)

    cs = WKV_CHUNK
    npair = RW_HEADS // 2
    sgn = 1 - 2 * d
    row = lax.broadcasted_iota(jnp.int32, (cs, 2 * cs), 0)
    col = lax.broadcasted_iota(jnp.int32, (cs, 2 * cs), 1) & (cs - 1)
    lag2 = (row - col) * sgn
    strict2 = lag2 > 0
    incl2 = lag2 >= 0
    m_incl = jnp.where(incl2[:, :cs], 1.0, 0.0).astype(BF16)
    eye = jnp.where(lax.broadcasted_iota(jnp.int32, (cs, cs), 0)
                    == lax.broadcasted_iota(jnp.int32, (cs, cs), 1), 1.0, 0.0)
    lane = lax.broadcasted_iota(jnp.int32, (1, LANES), 1)
    left = lane < RW_HEAD
    right = lane >= RW_HEAD
    left2 = (lax.broadcasted_iota(jnp.int32, (cs, 2 * LANES), 1) & (LANES - 1)) < RW_HEAD
    blockdiag = (lax.broadcasted_iota(jnp.int32, (LANES, LANES), 0) // RW_HEAD
                 == lax.broadcasted_iota(jnp.int32, (LANES, LANES), 1) // RW_HEAD)
    eye_l = jnp.where(lax.broadcasted_iota(jnp.int32, (LANES, LANES), 0)
                      == lax.broadcasted_iota(jnp.int32, (LANES, LANES), 1), 1.0, 0.0).astype(BF16)
    zeros_cv = jnp.zeros((cs, LANES), BF16)
    zeros_cf = jnp.zeros((cs, LANES), F32)
    ka = ka_ref[...]

    offs, prep = [], []
    for j in range(nchunk):
        off = pl.multiple_of((j + d * (nchunk - 1 - 2 * j)) * cs, cs)
        offs.append(off)
        ld = ld_ref[0, 0, pl.ds(off, cs), :]
        cum = _sel_left(m_incl, ld)
        tot = jnp.sum(ld, axis=0, keepdims=True)
        r = r_ref[0, pl.ds(off, cs), :].astype(F32)
        k = k_ref[0, pl.ds(off, cs), :].astype(F32)
        kn = kn_ref[0, pl.ds(off, cs), :].astype(F32)
        a = a_ref[0, 0, pl.ds(off, cs), :]
        kdir = k * (1.0 + (a - 1.0) * ka)
        b = kn * a
        e_ncum = jnp.exp(-cum)
        e_rem = jnp.exp(tot - cum)
        prep.append(dict(
            rt=(r * jnp.exp(cum)).astype(BF16), at=(-kn * jnp.exp(cum - ld)).astype(BF16),
            kt=(kdir * e_ncum).astype(BF16), bt=(b * e_ncum).astype(BF16),
            kh=(kdir * e_rem).astype(BF16), bh=(b * e_rem).astype(BF16),
            v=v_ref[0, pl.ds(off, cs), :], wc=jnp.exp(tot)))

    units = [(j, p) for j in range(nchunk) for p in range(npair)]
    heads = [(j, p, hh) for (j, p) in units for hh in range(2)]
    sl = lambda p: slice(p * LANES, (p + 1) * LANES)

    g_up, g_low, a_ab = {}, {}, {}
    for (j, p) in units:
        q = prep[j]
        ar = jnp.concatenate([q["at"][:, sl(p)], q["rt"][:, sl(p)]], axis=0)
        bk = jnp.concatenate([q["bt"][:, sl(p)], q["kt"][:, sl(p)]], axis=0)
        for hh in range(2):
            arm = jnp.where(left if hh == 0 else right, ar, jnp.zeros_like(ar))
            gm = _dot_nt(arm, bk)
            gu = jnp.where(strict2, gm[:cs], 0.0)
            a_ab[j, p, hh] = gu[:, :cs]
            g_up[j, p, hh] = gu.astype(BF16)
            g_low[j, p, hh] = jnp.where(incl2, gm[cs:], 0.0).astype(BF16)

    tinv = {h: a_ab[h] + eye for h in heads}
    pw = dict(a_ab)
    for _ in range(int(math.log2(cs)) - 1):
        for h in heads:
            pb = pw[h].astype(BF16)
            pw[h] = _dot(pb, pb)
        for h in heads:
            tinv[h] = tinv[h] + _dot(tinv[h].astype(BF16), pw[h].astype(BF16))

    w1 = {}
    for (j, p) in units:
        zv = jnp.concatenate([zeros_cv, prep[j]["v"][:, sl(p)]], axis=0)
        for hh in range(2):
            w1[j, p, hh] = _dot(g_up[j, p, hh], zv)
    tx = {}
    for (j, p, hh) in heads:
        rhs = jnp.concatenate([w1[j, p, hh].astype(BF16), prep[j]["at"][:, sl(p)]], axis=1)
        tx[j, p, hh] = _dot(tinv[j, p, hh].astype(BF16), rhs)
    lmat = {}
    for (j, p) in units:
        va = jnp.where(left2, tx[j, p, 0], tx[j, p, 1])
        vrow = jnp.concatenate([prep[j]["v"][:, sl(p)].astype(F32), zeros_cf], axis=1)
        lmat[j, p] = jnp.concatenate([va, vrow], axis=0).astype(BF16)

    z = {h: _dot(g_low[h], lmat[h[0], h[1]]) for h in heads}
    lv_t = {u: _dot_nt(eye_l, lmat[u][:, :LANES]).astype(BF16) for u in units}
    bh_t = {(j, p): _dot_nt(eye_l, prep[j]["bh"][:, sl(p)]).astype(BF16) for (j, p) in units}
    y0, m1, m2, n2t = {}, {}, {}, {}
    for (j, p) in units:
        q = prep[j]
        ym = jnp.where(left2, z[j, p, 0], z[j, p, 1])
        y0[j, p] = ym[:, :LANES]
        m1[j, p] = (q["rt"][:, sl(p)].astype(F32) + ym[:, LANES:]).astype(BF16)
        bkh = jnp.concatenate([q["bh"][:, sl(p)], q["kh"][:, sl(p)]], axis=0)
        n2t[j, p] = jnp.where(blockdiag, _dot(lv_t[j, p], bkh), 0.0)
    for (j, p) in units:
        m2[j, p] = jnp.where(blockdiag, _dot(bh_t[j, p], lmat[j, p][:cs, LANES:]), 0.0).astype(BF16)

    hts = [h_ref[p] for p in range(npair)]
    for j in range(nchunk):
        for p in range(npair):
            htb = hts[p].astype(BF16)
            y = y0[j, p] + _dot_nt(m1[j, p], htb)
            y_ref[0, 0, pl.ds(offs[j], cs), sl(p)] = y.astype(y_ref.dtype)
            hts[p] = hts[p] * prep[j]["wc"][:, sl(p)] + _dot_nt(htb, m2[j, p]) + n2t[j, p]
    for p in range(npair):
        h_ref[p] = hts[p]


def _wkv(r, k, v, kn, ld, a, k_a, nchunk=2):
    n, t, w = r.shape
    tb = WKV_CHUNK * nchunk
    nb = t // tb

    def tmap(b, d, c):
        return (b, c + d * (nb - 1 - 2 * c), 0)

    def dmap(b, d, c):
        return (d, b, c + d * (nb - 1 - 2 * c), 0)

    tile = pl.BlockSpec((1, tb, w), tmap)
    dtile = pl.BlockSpec((1, 1, tb, w), dmap)
    return pl.pallas_call(
        functools.partial(_wkv_kernel, nchunk=nchunk),
        grid=(n, 2, nb),
        in_specs=[tile, tile, tile, tile, dtile, dtile, _resident(k_a.shape)],
        out_specs=dtile,
        out_shape=jax.ShapeDtypeStruct((2, n, t, w), BF16),
        scratch_shapes=[pltpu.VMEM((RW_HEADS // 2, LANES, LANES), F32)],
        compiler_params=_cparams(("parallel", "arbitrary", "arbitrary")),
        name="wkv7_scan",
    )(r, k, v, kn, ld, a, k_a)


def _conv_kernel(x_ref, xp_ref, xn_ref, w_ref, b_ref, o_ref):
    i = pl.program_id(1)
    nt = pl.num_programs(1)
    x = x_ref[0].astype(F32)
    prev_rows = jnp.where(i > 0, xp_ref[0].astype(F32), 0.0)
    next_rows = jnp.where(i < nt - 1, xn_ref[0].astype(F32), 0.0)
    half = CONV_W // 2
    acc = x * w_ref[half:half + 1, :] + b_ref[...]
    for j in range(CONV_W):
        if j != half:
            acc = acc + _shifted(x, prev_rows, next_rows, j - half) * w_ref[j:j + 1, :]
    o_ref[0] = (acc * _sigmoid(acc)).astype(o_ref.dtype)


def _conv_silu(xbc, w, b, tt=256, tc=1024):
    n, t, c = xbc.shape
    tt = min(tt, t)
    per = tt // HALO
    nrow = t // HALO
    return pl.pallas_call(
        _conv_kernel,
        grid=(n, t // tt, c // tc),
        in_specs=[pl.BlockSpec((1, tt, tc), lambda b_, i, j: (b_, i, j)),
                  pl.BlockSpec((1, HALO, tc), lambda b_, i, j: (b_, jnp.maximum(i * per - 1, 0), j)),
                  pl.BlockSpec((1, HALO, tc), lambda b_, i, j: (b_, jnp.minimum((i + 1) * per, nrow - 1), j)),
                  pl.BlockSpec((CONV_W, tc), lambda b_, i, j: (0, j)),
                  pl.BlockSpec((1, tc), lambda b_, i, j: (0, j))],
        out_specs=pl.BlockSpec((1, tt, tc), lambda b_, i, j: (b_, i, j)),
        out_shape=jax.ShapeDtypeStruct((n, t, c), BF16),
        compiler_params=_cparams(("parallel", "parallel", "parallel")),
        name="mamba_conv",
    )(xbc, xbc, xbc, w, b)


def _ssd_kernel(xbc_ref, dt_ref, dtb_ref, alog_ref, hexp_ref, y_ref, h_ref):
    d = pl.program_id(1)
    c = pl.program_id(2)

    @pl.when(c == 0)
    def _():
        h_ref[...] = jnp.zeros_like(h_ref)

    ln = SSD_CHUNK
    row = lax.broadcasted_iota(jnp.int32, (ln, ln), 0)
    col = lax.broadcasted_iota(jnp.int32, (ln, ln), 1)
    incl = (row - col) * (1 - 2 * d) >= 0
    m_incl = jnp.where(incl, 1.0, 0.0).astype(BF16)

    dt = _softplus(dt_ref[0] + dtb_ref[0])
    da = dt * (-jnp.exp(alog_ref[0]))
    cum = _sel_left(m_incl, da)
    tot = jnp.sum(da, axis=0, keepdims=True)
    cum_t = cum.T
    hexp = hexp_ref[...]
    dt_x = _dot(dt.astype(BF16), hexp)
    dst_x = _dot((jnp.exp(tot - cum) * dt).astype(BF16), hexp)
    ecum_x = _dot(jnp.exp(cum).astype(BF16), hexp)
    cdec = jnp.exp(jnp.broadcast_to(tot, (8, LANES)))
    cdec_x = _sel_right(cdec, hexp, pieces=3)[0:1]

    x = xbc_ref[0, :, 0:M_INNER].astype(F32)
    xdt = (x * dt_x).astype(BF16)
    xdst = (x * dst_x).astype(BF16)

    lane = lax.broadcasted_iota(jnp.int32, (1, LANES), 1)
    left = lane < M_HEADDIM
    eye_l = jnp.where(row == col, 1.0, 0.0).astype(BF16)
    gw = M_INNER // M_GROUPS
    hpg = M_HEADS // M_GROUPS
    b_off = M_INNER
    c_off = M_INNER + M_GROUPS * M_STATE
    groups = range(M_GROUPS)
    gsl = lambda g: slice(g * gw, (g + 1) * gw)
    bg = {g: xbc_ref[0, :, b_off + g * M_STATE:b_off + (g + 1) * M_STATE] for g in groups}
    cg = {g: xbc_ref[0, :, c_off + g * M_STATE:c_off + (g + 1) * M_STATE] for g in groups}
    cb = {g: _dot_nt(cg[g], bg[g]) for g in groups}
    hg = {g: h_ref[g] for g in groups}
    y_in = {g: _dot(cg[g], hg[g].astype(BF16)) * ecum_x[:, gsl(g)] for g in groups}
    bg_t = {g: _dot_nt(eye_l, bg[g]).astype(BF16) for g in groups}
    for g in groups:
        h_ref[g] = hg[g] * cdec_x[:, gsl(g)] + _dot(bg_t[g], xdst[:, gsl(g)])
    for g in groups:
        for pr in range(hpg // 2):
            lo = g * gw + pr * LANES
            ys = []
            for hh in range(2):
                e = g * hpg + 2 * pr + hh
                seg = cum[:, e:e + 1] - cum_t[e:e + 1, :]
                decay = jnp.exp(jnp.where(incl, seg, -1e30))
                ys.append(_dot((cb[g] * decay).astype(BF16), xdt[:, lo:lo + LANES]))
            y = jnp.where(left, ys[0], ys[1]) + y_in[g][:, pr * LANES:(pr + 1) * LANES]
            y_ref[0, 0, :, lo:lo + LANES] = y.astype(y_ref.dtype)


def _ssd(xbc_act, dt_raw, dt_bias, a_log, hexp):
    n, t, c = xbc_act.shape
    ln = SSD_CHUNK
    nc = t // ln

    def tmap(b, d, ci):
        return (b, ci + d * (nc - 1 - 2 * ci), 0)

    def dmap(b, d, ci):
        return (d, b, ci + d * (nc - 1 - 2 * ci), 0)

    par = pl.BlockSpec((1, 1, LANES), lambda b, d, ci: (d, 0, 0))
    return pl.pallas_call(
        _ssd_kernel,
        grid=(n, 2, nc),
        in_specs=[pl.BlockSpec((1, ln, c), tmap), pl.BlockSpec((1, ln, LANES), tmap), par, par,
                  _resident(hexp.shape)],
        out_specs=pl.BlockSpec((1, 1, ln, M_INNER), dmap),
        out_shape=jax.ShapeDtypeStruct((2, n, t, M_INNER), BF16),
        scratch_shapes=[pltpu.VMEM((M_GROUPS, M_STATE, M_INNER // M_GROUPS), F32)],
        compiler_params=_cparams(("parallel", "arbitrary", "arbitrary")),
        name="ssd_scan",
    )(xbc_act, dt_raw, dt_bias, a_log, hexp)


def _merge_kernel(yw_ref, bonus_ref, g_ref, ym_ref, xs_ref, z_ref, gates_ref, x_ref,
                  hdown_ref, hup_ref, gng_ref, gnb_ref, dskip_ref, ng_ref, wbr_ref, wbm_ref, wo_ref,
                  lng_ref, lnb_ref, o_ref):
    y = yw_ref[0, 0].astype(F32) + yw_ref[1, 0].astype(F32)
    hdown = hdown_ref[...]
    hup = hup_ref[...]
    mu = _head_sum(y, hdown, hup) * (1.0 / RW_HEAD)
    yc = y - mu
    var = _head_sum(yc * yc, hdown, hup) * (1.0 / RW_HEAD)
    yn = yc * lax.rsqrt(var + GN_EPS) * gng_ref[...] + gnb_ref[...]
    rw = (yn + bonus_ref[0].astype(F32)) * g_ref[0].astype(F32)
    z = z_ref[0].astype(F32)
    ym = ym_ref[0, 0].astype(F32) + ym_ref[1, 0].astype(F32) + dskip_ref[...] * xs_ref[0].astype(F32)
    ym = ym * (z * _sigmoid(z))
    gw = M_INNER // M_GROUPS
    parts = []
    for gi in range(M_GROUPS):
        yg = ym[:, gi * gw:(gi + 1) * gw]
        ms = jnp.mean(yg * yg, axis=-1, keepdims=True)
        parts.append(yg * lax.rsqrt(ms + LN_EPS))
    mo = jnp.concatenate(parts, axis=1) * ng_ref[...]
    u_rw = _dot(rw.astype(BF16), wbr_ref[...])
    u_m = _dot(mo.astype(BF16), wbm_ref[...])
    gates = gates_ref[0]
    mixed = gates[:, :D_MODEL].astype(F32) * u_rw + gates[:, D_MODEL:].astype(F32) * u_m
    mix = _dot(mixed.astype(BF16), wo_ref[...])
    o_ref[0] = _layer_norm(ALPHA * x_ref[0] + mix, lng_ref[...], lnb_ref[...])


def _merge(yw, bonus, g, ym, xbc_act, z, gates, x, hdown, hup, gn_g, gn_b, dskip, norm_g, w_br, w_bm, w_o,
           ln_g, ln_b, tm=256):
    n, t, _ = x.shape
    tm = min(tm, t)
    tile = lambda width: pl.BlockSpec((1, tm, width), lambda b, i: (b, i, 0))
    dtile = lambda width: pl.BlockSpec((2, 1, tm, width), lambda b, i: (0, b, i, 0))
    res = [hdown, hup, gn_g, gn_b, dskip, norm_g, w_br, w_bm, w_o, ln_g, ln_b]
    return pl.pallas_call(
        _merge_kernel,
        grid=(n, t // tm),
        in_specs=[dtile(RW_WIDTH), tile(RW_WIDTH), tile(RW_WIDTH), dtile(M_INNER), tile(M_INNER),
                  tile(M_INNER), tile(2 * D_MODEL), tile(D_MODEL)] + [_resident(a.shape) for a in res],
        out_specs=tile(D_MODEL),
        out_shape=jax.ShapeDtypeStruct(x.shape, F32),
        compiler_params=_cparams(("parallel", "parallel")),
        name="merge_ln1",
    )(yw, bonus, g, ym, xbc_act, z, gates, x, *res)


def _attn_kernel(x_ref, kv_ref, wq_ref, wco_ref, lng_ref, lnb_ref, o_ref):
    x = x_ref[0]
    q = _dot(x.astype(BF16), wq_ref[...]).astype(BF16)
    scale = 1.0 / math.sqrt(X_HEAD_DIM)
    out = None
    for h in range(X_HEADS):
        sl = slice(h * X_HEAD_DIM, (h + 1) * X_HEAD_DIM)
        s = _dot_nt(q[:, sl], kv_ref[0, :, sl]) * scale
        s = s - jnp.max(s, axis=-1, keepdims=True)
        e = jnp.exp(s)
        pr = e / jnp.sum(e, axis=-1, keepdims=True)
        o = _dot(pr.astype(BF16), kv_ref[0, :, D_MODEL + h * X_HEAD_DIM:D_MODEL + (h + 1) * X_HEAD_DIM])
        c = _dot(o.astype(BF16), wco_ref[sl, :])
        out = c if out is None else out + c
    o_ref[0] = _layer_norm(ALPHA * x + out, lng_ref[...], lnb_ref[...])


def _attention(x, kv, w_q, w_co, ln_g, ln_b, tm=512):
    n, t, _ = x.shape
    tm = min(tm, t)
    tile = pl.BlockSpec((1, tm, D_MODEL), lambda b, i: (b, i, 0))
    return pl.pallas_call(
        _attn_kernel,
        grid=(n, t // tm),
        in_specs=[tile, pl.BlockSpec((1, N_MEM, 2 * D_MODEL), lambda b, i: (b, 0, 0)),
                  _resident(w_q.shape), _resident(w_co.shape), _resident(ln_g.shape), _resident(ln_b.shape)],
        out_specs=tile,
        out_shape=jax.ShapeDtypeStruct(x.shape, F32),
        compiler_params=_cparams(("parallel", "parallel")),
        name="mem_attention_ln2",
    )(x, kv, w_q, w_co, ln_g, ln_b)


def _mlp_kernel(x_ref, wu_ref, wd_ref, lng_ref, lnb_ref, o_ref, acc_ref, xb_ref):
    f = pl.program_id(1)

    @pl.when(f == 0)
    def _():
        xb_ref[...] = x_ref[...].astype(BF16)
        acc_ref[...] = jnp.zeros_like(acc_ref)

    h = jnp.maximum(_dot(xb_ref[...], wu_ref[...]), 0.0)
    acc_ref[...] += _dot((h * h).astype(BF16), wd_ref[...])

    @pl.when(f == pl.num_programs(1) - 1)
    def _():
        o_ref[...] = _layer_norm(ALPHA * x_ref[...] + acc_ref[...], lng_ref[...], lnb_ref[...])


def _mlp(x, w_up, w_down, ln_g, ln_b, tm=512, tf=1024):
    m, dm = x.shape
    tm = min(tm, m)
    return pl.pallas_call(
        _mlp_kernel,
        grid=(m // tm, D_FF // tf),
        in_specs=[pl.BlockSpec((tm, dm), lambda i, f: (i, 0)),
                  pl.BlockSpec((dm, tf), lambda i, f: (0, f)),
                  pl.BlockSpec((tf, dm), lambda i, f: (f, 0)),
                  _resident(ln_g.shape), _resident(ln_b.shape)],
        out_specs=pl.BlockSpec((tm, dm), lambda i, f: (i, 0)),
        out_shape=jax.ShapeDtypeStruct((m, dm), F32),
        scratch_shapes=[pltpu.VMEM((tm, dm), F32), pltpu.VMEM((tm, dm), BF16)],
        compiler_params=_cparams(("parallel", "arbitrary")),
        name="mlp_ln3",
    )(x, w_up, w_down, ln_g, ln_b)


def _pad_cols(a, width):
    return jnp.pad(a, [(0, 0)] * (a.ndim - 1) + [(0, width - a.shape[-1])])


def _rw_cols(a):
    w3 = 3 * RW_WIDTH
    return jnp.concatenate([a[..., :w3],
                            _pad_cols(a[..., w3:w3 + DECAY_LORA], LORA_PAD),
                            _pad_cols(a[..., w3 + DECAY_LORA:w3 + DECAY_LORA + ICLR_LORA], LORA_PAD),
                            a[..., w3 + DECAY_LORA + ICLR_LORA:RW_SHIFT_COLS]], axis=-1)


def _prepare(w_in, rw_mu_prev, rw_mu_next, rw_w0, rw_w2, rw_a0, rw_a2, rw_g2, rw_k_k, rw_k_a, rw_r_k,
             rw_gn_g, rw_gn_b, m_conv_w, m_conv_b, m_dt_bias, m_a_log, m_d, m_norm_g, w_br, w_bm, w_o,
             ln1_g, ln1_b, w_q, w_kv, w_co, ln2_g, ln2_b, w_up, w_down, ln3_g, ln3_b):
    l = 0
    w = w_in[l]
    c0 = RW_SHIFT_COLS
    c1 = c0 + M_INNER
    c2 = c1 + M_CONV_CH
    c3 = c2 + M_HEADS
    row = lambda a: a.reshape(1, -1).astype(F32)
    head = jnp.arange(RW_WIDTH) // RW_HEAD
    lane_head = jnp.arange(M_INNER) // M_HEADDIM
    return dict(
        w_rw=_rw_cols(w[:, :c0]).astype(BF16),
        w_z=w[:, c0:c1].astype(BF16),
        w_xbc=w[:, c1:c2].astype(BF16),
        w_dt=_pad_cols(w[:, c2:c3], LANES).astype(BF16),
        w_gates=w[:, c3:].astype(BF16),
        mup=row(_rw_cols(rw_mu_prev[l])), mun=row(_rw_cols(rw_mu_next[l])),
        w0=rw_w0[l].astype(F32), a0=rw_a0[l].astype(F32),
        w2=jnp.pad(rw_w2[l], ((0, LORA_PAD - DECAY_LORA), (0, 0))).astype(BF16),
        a2=jnp.pad(rw_a2[l], ((0, LORA_PAD - ICLR_LORA), (0, 0))).astype(BF16),
        g2=rw_g2[l].astype(BF16),
        k_k=row(rw_k_k[l]), k_a=row(rw_k_a[l]), r_k=row(rw_r_k[l]),
        gn_g=row(rw_gn_g[l]), gn_b=row(rw_gn_b[l]),
        hdown=(head[:, None] == jnp.arange(LANES)[None, :]).astype(BF16),
        hup=(jnp.arange(LANES)[:, None] == head[None, :]).astype(BF16),
        conv_w=m_conv_w[l].astype(F32), conv_b=row(m_conv_b[l]),
        dt_bias=_pad_cols(m_dt_bias[l], LANES).reshape(2, 1, LANES).astype(F32),
        a_log=_pad_cols(m_a_log[l], LANES).reshape(2, 1, LANES).astype(F32),
        hexp=(jnp.arange(LANES)[:, None] == lane_head[None, :]).astype(BF16),
        dskip=row(jnp.repeat(m_d[l], M_HEADDIM)), norm_g=row(m_norm_g[l]),
        w_br=w_br[l].astype(BF16), w_bm=w_bm[l].astype(BF16), w_o=w_o[l].astype(BF16),
        ln1_g=row(ln1_g[l]), ln1_b=row(ln1_b[l]),
        w_q=w_q[l].astype(BF16), w_kv=w_kv[l].astype(BF16), w_co=w_co[l].astype(BF16),
        ln2_g=row(ln2_g[l]), ln2_b=row(ln2_b[l]),
        w_up=w_up[l].astype(BF16), w_down=w_down[l].astype(BF16),
        ln3_g=row(ln3_g[l]), ln3_b=row(ln3_b[l]),
    )


def _encoder_layer(x, mem, p):
    n, t, dm = x.shape
    x2 = x.reshape(n * t, dm)
    as3 = lambda a: a.reshape(n, t, a.shape[-1])
    p_rw = as3(_mm(x2, p["w_rw"]))
    z = as3(_mm(x2, p["w_z"]))
    xbc = as3(_mm(x2, p["w_xbc"]))
    dt_raw = as3(_mm(x2, p["w_dt"], out_dtype=F32))
    gates = as3(_mm(x2, p["w_gates"], act="sigmoid"))

    r, k, v, kn, g, bonus, ld, a = _rw_prep(p_rw, p["mup"], p["mun"], p["w2"], p["a2"], p["g2"],
                                            p["w0"], p["a0"], p["k_k"], p["r_k"], p["hdown"], p["hup"])
    yw = _wkv(r, k, v, kn, ld, a, p["k_a"])

    xbc_act = _conv_silu(xbc, p["conv_w"], p["conv_b"])
    ym = _ssd(xbc_act, dt_raw, p["dt_bias"], p["a_log"], p["hexp"])

    x1 = _merge(yw, bonus, g, ym, xbc_act, z, gates, x, p["hdown"], p["hup"], p["gn_g"], p["gn_b"], p["dskip"],
                p["norm_g"], p["w_br"], p["w_bm"], p["w_o"], p["ln1_g"], p["ln1_b"])

    kv = _mm(mem.reshape(n * N_MEM, dm), p["w_kv"]).reshape(n, N_MEM, 2 * dm)
    x2_ = _attention(x1, kv, p["w_q"], p["w_co"], p["ln2_g"], p["ln2_b"])
    out = _mlp(x2_.reshape(n * t, dm), p["w_up"], p["w_down"], p["ln3_g"], p["ln3_b"])
    return out.reshape(n, t, dm)


def kernel(x_prompt, x_sample, mem_prompt, mem_sample, w_in, rw_mu_prev, rw_mu_next, rw_w0, rw_w2, rw_a0, rw_a2, rw_g2, rw_k_k, rw_k_a, rw_r_k, rw_gn_g, rw_gn_b, m_conv_w, m_conv_b, m_dt_bias, m_a_log, m_d, m_norm_g, w_br, w_bm, w_o, ln1_g, ln1_b, w_q, w_kv, w_co, ln2_g, ln2_b, w_up, w_down, ln3_g, ln3_b):
    p = _prepare(w_in, rw_mu_prev, rw_mu_next, rw_w0, rw_w2, rw_a0, rw_a2, rw_g2, rw_k_k, rw_k_a, rw_r_k,
                 rw_gn_g, rw_gn_b, m_conv_w, m_conv_b, m_dt_bias, m_a_log, m_d, m_norm_g, w_br, w_bm, w_o,
                 ln1_g, ln1_b, w_q, w_kv, w_co, ln2_g, ln2_b, w_up, w_down, ln3_g, ln3_b)
    return (_encoder_layer(x_prompt, mem_prompt, p), _encoder_layer(x_sample, mem_sample, p))
```

```python
import functools
import math

import jax
import jax.numpy as jnp
from jax import lax
from jax.experimental import pallas as pl
from jax.experimental.pallas import tpu as pltpu

F32 = jnp.float32
BF16 = jnp.bfloat16

D_MODEL = 2048
RW_HEAD = 64
RW_WIDTH = D_MODEL // 2
RW_HEADS = RW_WIDTH // RW_HEAD
DECAY_LORA = 96
ICLR_LORA = 96
GATE_LORA = 256
LORA_PAD = 128
M_INNER = D_MODEL
M_HEADDIM = 64
M_HEADS = M_INNER // M_HEADDIM
M_STATE = 128
M_GROUPS = 8
CONV_W = 5
N_MEM = 256
X_HEADS = 4
X_HEAD_DIM = D_MODEL // X_HEADS
D_FF = 4 * D_MODEL
DEPTH = 1
ALPHA = (2.0 * DEPTH) ** 0.25
LN_EPS = 1e-5
GN_EPS = 64e-5

RW_SHIFT_COLS = 3 * RW_WIDTH + DECAY_LORA + ICLR_LORA + GATE_LORA
RW_PAD_COLS = 3 * RW_WIDTH + 2 * LORA_PAD + GATE_LORA
M_CONV_CH = M_INNER + 2 * M_GROUPS * M_STATE

WKV_CHUNK = 64
SSD_CHUNK = 128
HALO = 16
LANES = 128
VMEM_LIMIT = 56 * 1024 * 1024


def _cparams(sem):
    return pltpu.CompilerParams(dimension_semantics=sem, vmem_limit_bytes=VMEM_LIMIT)


def _resident(shape):
    nd = len(shape)
    return pl.BlockSpec(shape, lambda *_: (0,) * nd, pipeline_mode=pl.Buffered(1))


def _split3(a):
    a1 = a.astype(BF16)
    r1 = a - a1.astype(F32)
    a2 = r1.astype(BF16)
    a3 = (r1 - a2.astype(F32)).astype(BF16)
    return a1, a2, a3


def _dot(a, b):
    return jnp.dot(a, b, preferred_element_type=F32)


def _dot_nt(a, b):
    return lax.dot_general(a, b, (((1,), (1,)), ((), ())), preferred_element_type=F32)


def _sel_left(m01, a):
    a1, a2, a3 = _split3(a)
    return _dot(m01, a1) + _dot(m01, a2) + _dot(m01, a3)


def _sel_right(a, m01, pieces=2):
    a1, a2, a3 = _split3(a)
    out = _dot(a1, m01) + _dot(a2, m01)
    if pieces == 3:
        out = out + _dot(a3, m01)
    return out


def _head_sum(a, down, up):
    return _sel_right(_sel_right(a, down), up)


def _softplus(x):
    return jnp.maximum(x, 0.0) + jnp.log1p(jnp.exp(-jnp.abs(x)))


def _sigmoid(x):
    return 1.0 / (1.0 + jnp.exp(-x))


def _layer_norm(x, g, b):
    mu = jnp.mean(x, axis=-1, keepdims=True)
    xc = x - mu
    var = jnp.mean(xc * xc, axis=-1, keepdims=True)
    return xc * lax.rsqrt(var + LN_EPS) * g + b


def _mm_kernel(x_ref, w_ref, o_ref, xb_ref):
    @pl.when(pl.program_id(1) == 0)
    def _():
        xb_ref[...] = x_ref[...].astype(BF16)

    o_ref[...] = _dot(xb_ref[...], w_ref[...]).astype(o_ref.dtype)


def _mm(x, w, *, out_dtype=BF16, tm=1024, tn=512):
    m, k = x.shape
    n = w.shape[1]
    tm = min(tm, m)
    tn = min(tn, n)
    return pl.pallas_call(
        _mm_kernel,
        grid=(m // tm, n // tn),
        in_specs=[pl.BlockSpec((tm, k), lambda i, j: (i, 0)),
                  pl.BlockSpec((k, tn), lambda i, j: (0, j))],
        out_specs=pl.BlockSpec((tm, tn), lambda i, j: (i, j)),
        out_shape=jax.ShapeDtypeStruct((m, n), out_dtype),
        scratch_shapes=[pltpu.VMEM((tm, k), BF16)],
        compiler_params=_cparams(("parallel", "arbitrary")),
        name="matmul",
    )(x, w)


INPROJ_TN = 1024
COL_RW = 0
COL_XBC = 4 * INPROJ_TN
COL_GATES = COL_XBC + M_CONV_CH
COL_Z = COL_GATES + 2 * D_MODEL
INPROJ_COLS = COL_Z + M_INNER


def _inproj_kernel(x_ref, w_ref, wdt_ref, o_ref, dt_ref, xb_ref):
    j = pl.program_id(1)

    @pl.when(j == 0)
    def _():
        xb = x_ref[...].astype(BF16)
        xb_ref[...] = xb
        dt_ref[...] = _dot(xb, wdt_ref[...])

    acc = _dot(xb_ref[...], w_ref[0])
    is_gate = jnp.logical_and(j >= COL_GATES // INPROJ_TN, j < COL_Z // INPROJ_TN)

    @pl.when(is_gate)
    def _():
        o_ref[...] = _sigmoid(acc).astype(o_ref.dtype)

    @pl.when(jnp.logical_not(is_gate))
    def _():
        o_ref[...] = acc.astype(o_ref.dtype)


def _inproj(x, w_tiles, w_dt, tm=1024):
    m, k = x.shape
    tm = min(tm, m)
    tn = INPROJ_TN
    return pl.pallas_call(
        _inproj_kernel,
        grid=(m // tm, INPROJ_COLS // tn),
        in_specs=[pl.BlockSpec((tm, k), lambda i, j: (i, 0)),
                  pl.BlockSpec((1, k, tn), lambda i, j: (j, 0, 0)),
                  _resident(w_dt.shape)],
        out_specs=[pl.BlockSpec((tm, tn), lambda i, j: (i, j)),
                   pl.BlockSpec((tm, LANES), lambda i, j: (i, 0))],
        out_shape=[jax.ShapeDtypeStruct((m, INPROJ_COLS), BF16), jax.ShapeDtypeStruct((m, LANES), F32)],
        scratch_shapes=[pltpu.VMEM((tm, k), BF16)],
        compiler_params=_cparams(("parallel", "arbitrary")),
        name="in_projection",
    )(x, w_tiles, w_dt)


def _halo_specs(t, tt, width):
    per = tt // HALO
    last = t // HALO - 1

    def prev_map(b, i):
        return (b, jnp.maximum(i * per - 1, 0), 0)

    def next_map(b, i):
        return (b, jnp.minimum((i + 1) * per, last), 0)

    return (pl.BlockSpec((1, HALO, width), prev_map), pl.BlockSpec((1, HALO, width), next_map))


def _shifted(x, prev_rows, next_rows, off):
    tt = x.shape[0]
    rows = lax.broadcasted_iota(jnp.int32, (tt, 1), 0)
    if off < 0:
        out = pltpu.roll(x, -off, 0)
        for j in range(-off):
            out = jnp.where(rows == j, prev_rows[HALO + off + j:HALO + off + j + 1, :], out)
    else:
        out = pltpu.roll(x, tt - off, 0)
        for j in range(off):
            out = jnp.where(rows == tt - off + j, next_rows[j:j + 1, :], out)
    return out


def _rw_prep_kernel(p_ref, pp_ref, pn_ref, mup_ref, mun_ref, w2_ref, a2_ref, g2_ref, w0_ref, a0_ref,
                    kk_ref, rk_ref, hdown_ref, hup_ref,
                    r_out, k_out, v_out, kn_out, g_out, bonus_out, ld_out, a_out):
    i = pl.program_id(1)
    nt = pl.num_programs(1)
    p = p_ref[0].astype(F32)
    prev_rows = jnp.where(i > 0, pp_ref[0].astype(F32), 0.0)
    next_rows = jnp.where(i < nt - 1, pn_ref[0].astype(F32), 0.0)
    p_prev = _shifted(p, prev_rows, next_rows, -1)
    p_next = _shifted(p, prev_rows, next_rows, 1)
    ps = p + mup_ref[...] * (p_prev - p) + mun_ref[...] * (p_next - p)
    w = RW_WIDTH
    r = ps[:, 0:w]
    k = ps[:, w:2 * w]
    v = ps[:, 2 * w:3 * w]
    dw = ps[:, 3 * w:3 * w + LORA_PAD]
    da = ps[:, 3 * w + LORA_PAD:3 * w + 2 * LORA_PAD]
    dg = ps[:, 3 * w + 2 * LORA_PAD:]
    hw = _dot(jnp.tanh(dw).astype(BF16), w2_ref[...])
    ha = _dot(da.astype(BF16), a2_ref[...])
    g = _dot(_sigmoid(dg).astype(BF16), g2_ref[...])
    for d in range(2):
        logw = -_softplus(-(w0_ref[d:d + 1, :] + hw)) - 0.5
        ld_out[d, 0] = -jnp.exp(logw)
        a_out[d, 0] = _sigmoid(a0_ref[d:d + 1, :] + ha)
    hdown = hdown_ref[...]
    hup = hup_ref[...]
    kkr = k * kk_ref[...]
    ss = _head_sum(kkr * kkr, hdown, hup)
    kn = kkr * lax.rsqrt(jnp.maximum(ss, 1e-24))
    rk = _head_sum(r * k * rk_ref[...], hdown, hup)
    r_out[0] = r.astype(r_out.dtype)
    k_out[0] = k.astype(k_out.dtype)
    v_out[0] = v.astype(v_out.dtype)
    kn_out[0] = kn.astype(kn_out.dtype)
    g_out[0] = g.astype(g_out.dtype)
    bonus_out[0] = (rk * v).astype(bonus_out.dtype)


def _rw_prep(p, mup, mun, w2, a2, g2, w0, a0, k_k, r_k, hdown, hup, tt=256):
    n, t, _ = p.shape
    c = RW_PAD_COLS
    tt = min(tt, t)
    w = RW_WIDTH
    tile = lambda width: pl.BlockSpec((1, tt, width), lambda b, i: (b, i, 0))
    dir_tile = pl.BlockSpec((2, 1, tt, w), lambda b, i: (0, b, i, 0))
    hp, hn = _halo_specs(t, tt, c)
    bf = jax.ShapeDtypeStruct((n, t, w), BF16)
    dd = jax.ShapeDtypeStruct((2, n, t, w), F32)
    return pl.pallas_call(
        _rw_prep_kernel,
        grid=(n, t // tt),
        in_specs=[tile(c), hp, hn, _resident(mup.shape), _resident(mun.shape), _resident(w2.shape),
                  _resident(a2.shape), _resident(g2.shape), _resident(w0.shape), _resident(a0.shape),
                  _resident(k_k.shape), _resident(r_k.shape), _resident(hdown.shape), _resident(hup.shape)],
        out_specs=[tile(w)] * 6 + [dir_tile, dir_tile],
        out_shape=[bf] * 6 + [dd, dd],
        compiler_params=_cparams(("parallel", "parallel")),
        name="rwkv_prep",
    )(p, p, p, mup, mun, w2, a2, g2, w0, a0, k_k, r_k, hdown, hup)


def _wkv_kernel(r_ref, k_ref, v_ref, kn_ref, ld_ref, a_ref, ka_ref, y_ref, h_ref, *, nchunk):
    d = pl.program_id(1)
    c = pl.program_id(2)

    @pl.when(c == 0)
    def _():
        h_ref[...] = jnp.zeros_like(h_ref)

    cs = WKV_CHUNK
    npair = RW_HEADS // 2
    sgn = 1 - 2 * d
    row = lax.broadcasted_iota(jnp.int32, (cs, 2 * cs), 0)
    col = lax.broadcasted_iota(jnp.int32, (cs, 2 * cs), 1) & (cs - 1)
    lag2 = (row - col) * sgn
    strict2 = lag2 > 0
    incl2 = lag2 >= 0
    m_incl = jnp.where(incl2[:, :cs], 1.0, 0.0).astype(BF16)
    eye = jnp.where(lax.broadcasted_iota(jnp.int32, (cs, cs), 0)
                    == lax.broadcasted_iota(jnp.int32, (cs, cs), 1), 1.0, 0.0)
    lane = lax.broadcasted_iota(jnp.int32, (1, LANES), 1)
    left = lane < RW_HEAD
    right = lane >= RW_HEAD
    left2 = (lax.broadcasted_iota(jnp.int32, (cs, 2 * LANES), 1) & (LANES - 1)) < RW_HEAD
    blockdiag = (lax.broadcasted_iota(jnp.int32, (LANES, LANES), 0) // RW_HEAD
                 == lax.broadcasted_iota(jnp.int32, (LANES, LANES), 1) // RW_HEAD)
    eye_l = jnp.where(lax.broadcasted_iota(jnp.int32, (LANES, LANES), 0)
                      == lax.broadcasted_iota(jnp.int32, (LANES, LANES), 1), 1.0, 0.0).astype(BF16)
    zeros_cv = jnp.zeros((cs, LANES), BF16)
    zeros_cf = jnp.zeros((cs, LANES), F32)
    ka = ka_ref[...]

    offs, prep = [], []
    for j in range(nchunk):
        off = pl.multiple_of((j + d * (nchunk - 1 - 2 * j)) * cs, cs)
        offs.append(off)
        ld = ld_ref[0, 0, pl.ds(off, cs), :]
        cum = _sel_left(m_incl, ld)
        tot = jnp.sum(ld, axis=0, keepdims=True)
        r = r_ref[0, pl.ds(off, cs), :].astype(F32)
        k = k_ref[0, pl.ds(off, cs), :].astype(F32)
        kn = kn_ref[0, pl.ds(off, cs), :].astype(F32)
        a = a_ref[0, 0, pl.ds(off, cs), :]
        kdir = k * (1.0 + (a - 1.0) * ka)
        b = kn * a
        e_ncum = jnp.exp(-cum)
        e_rem = jnp.exp(tot - cum)
        prep.append(dict(
            rt=(r * jnp.exp(cum)).astype(BF16), at=(-kn * jnp.exp(cum - ld)).astype(BF16),
            kt=(kdir * e_ncum).astype(BF16), bt=(b * e_ncum).astype(BF16),
            kh=(kdir * e_rem).astype(BF16), bh=(b * e_rem).astype(BF16),
            v=v_ref[0, pl.ds(off, cs), :], wc=jnp.exp(tot)))

    units = [(j, p) for j in range(nchunk) for p in range(npair)]
    heads = [(j, p, hh) for (j, p) in units for hh in range(2)]
    sl = lambda p: slice(p * LANES, (p + 1) * LANES)

    g_up, g_low, a_ab = {}, {}, {}
    for (j, p) in units:
        q = prep[j]
        ar = jnp.concatenate([q["at"][:, sl(p)], q["rt"][:, sl(p)]], axis=0)
        bk = jnp.concatenate([q["bt"][:, sl(p)], q["kt"][:, sl(p)]], axis=0)
        for hh in range(2):
            arm = jnp.where(left if hh == 0 else right, ar, jnp.zeros_like(ar))
            gm = _dot_nt(arm, bk)
            gu = jnp.where(strict2, gm[:cs], 0.0)
            a_ab[j, p, hh] = gu[:, :cs]
            g_up[j, p, hh] = gu.astype(BF16)
            g_low[j, p, hh] = jnp.where(incl2, gm[cs:], 0.0).astype(BF16)

    tinv = {h: a_ab[h] + eye for h in heads}
    pw = dict(a_ab)
    for _ in range(int(math.log2(cs)) - 1):
        for h in heads:
            pb = pw[h].astype(BF16)
            pw[h] = _dot(pb, pb)
        for h in heads:
            tinv[h] = tinv[h] + _dot(tinv[h].astype(BF16), pw[h].astype(BF16))

    w1 = {}
    for (j, p) in units:
        zv = jnp.concatenate([zeros_cv, prep[j]["v"][:, sl(p)]], axis=0)
        for hh in range(2):
            w1[j, p, hh] = _dot(g_up[j, p, hh], zv)
    tx = {}
    for (j, p, hh) in heads:
        rhs = jnp.concatenate([w1[j, p, hh].astype(BF16), prep[j]["at"][:, sl(p)]], axis=1)
        tx[j, p, hh] = _dot(tinv[j, p, hh].astype(BF16), rhs)
    lmat = {}
    for (j, p) in units:
        va = jnp.where(left2, tx[j, p, 0], tx[j, p, 1])
        vrow = jnp.concatenate([prep[j]["v"][:, sl(p)].astype(F32), zeros_cf], axis=1)
        lmat[j, p] = jnp.concatenate([va, vrow], axis=0).astype(BF16)

    z = {h: _dot(g_low[h], lmat[h[0], h[1]]) for h in heads}
    lv_t = {u: _dot_nt(eye_l, lmat[u][:, :LANES]).astype(BF16) for u in units}
    bh_t = {(j, p): _dot_nt(eye_l, prep[j]["bh"][:, sl(p)]).astype(BF16) for (j, p) in units}
    y0, m1, m2, n2t = {}, {}, {}, {}
    for (j, p) in units:
        q = prep[j]
        ym = jnp.where(left2, z[j, p, 0], z[j, p, 1])
        y0[j, p] = ym[:, :LANES]
        m1[j, p] = (q["rt"][:, sl(p)].astype(F32) + ym[:, LANES:]).astype(BF16)
        bkh = jnp.concatenate([q["bh"][:, sl(p)], q["kh"][:, sl(p)]], axis=0)
        n2t[j, p] = jnp.where(blockdiag, _dot(lv_t[j, p], bkh), 0.0)
    for (j, p) in units:
        m2[j, p] = jnp.where(blockdiag, _dot(bh_t[j, p], lmat[j, p][:cs, LANES:]), 0.0).astype(BF16)

    hts = [h_ref[p] for p in range(npair)]
    for j in range(nchunk):
        for p in range(npair):
            htb = hts[p].astype(BF16)
            y = y0[j, p] + _dot_nt(m1[j, p], htb)
            y_ref[0, 0, pl.ds(offs[j], cs), sl(p)] = y.astype(y_ref.dtype)
            hts[p] = hts[p] * prep[j]["wc"][:, sl(p)] + _dot_nt(htb, m2[j, p]) + n2t[j, p]
    for p in range(npair):
        h_ref[p] = hts[p]


def _wkv(r, k, v, kn, ld, a, k_a, nchunk=2):
    n, t, w = r.shape
    tb = WKV_CHUNK * nchunk
    nb = t // tb

    def tmap(b, d, c):
        return (b, c + d * (nb - 1 - 2 * c), 0)

    def dmap(b, d, c):
        return (d, b, c + d * (nb - 1 - 2 * c), 0)

    tile = pl.BlockSpec((1, tb, w), tmap)
    dtile = pl.BlockSpec((1, 1, tb, w), dmap)
    return pl.pallas_call(
        functools.partial(_wkv_kernel, nchunk=nchunk),
        grid=(n, 2, nb),
        in_specs=[tile, tile, tile, tile, dtile, dtile, _resident(k_a.shape)],
        out_specs=dtile,
        out_shape=jax.ShapeDtypeStruct((2, n, t, w), BF16),
        scratch_shapes=[pltpu.VMEM((RW_HEADS // 2, LANES, LANES), F32)],
        compiler_params=_cparams(("parallel", "arbitrary", "arbitrary")),
        name="wkv7_scan",
    )(r, k, v, kn, ld, a, k_a)


def _conv_kernel(x_ref, xp_ref, xn_ref, w_ref, b_ref, o_ref):
    i = pl.program_id(1)
    nt = pl.num_programs(1)
    x = x_ref[0].astype(F32)
    prev_rows = jnp.where(i > 0, xp_ref[0].astype(F32), 0.0)
    next_rows = jnp.where(i < nt - 1, xn_ref[0].astype(F32), 0.0)
    half = CONV_W // 2
    acc = x * w_ref[half:half + 1, :] + b_ref[...]
    for j in range(CONV_W):
        if j != half:
            acc = acc + _shifted(x, prev_rows, next_rows, j - half) * w_ref[j:j + 1, :]
    o_ref[0] = (acc * _sigmoid(acc)).astype(o_ref.dtype)


def _conv_silu(proj, w, b, tt=256, tc=1024):
    n, t, _ = proj.shape
    c = M_CONV_CH
    j0 = COL_XBC // tc
    tt = min(tt, t)
    per = tt // HALO
    nrow = t // HALO
    return pl.pallas_call(
        _conv_kernel,
        grid=(n, t // tt, c // tc),
        in_specs=[pl.BlockSpec((1, tt, tc), lambda b_, i, j: (b_, i, j0 + j)),
                  pl.BlockSpec((1, HALO, tc), lambda b_, i, j: (b_, jnp.maximum(i * per - 1, 0), j0 + j)),
                  pl.BlockSpec((1, HALO, tc), lambda b_, i, j: (b_, jnp.minimum((i + 1) * per, nrow - 1), j0 + j)),
                  pl.BlockSpec((CONV_W, tc), lambda b_, i, j: (0, j)),
                  pl.BlockSpec((1, tc), lambda b_, i, j: (0, j))],
        out_specs=pl.BlockSpec((1, tt, tc), lambda b_, i, j: (b_, i, j)),
        out_shape=jax.ShapeDtypeStruct((n, t, c), BF16),
        compiler_params=_cparams(("parallel", "parallel", "parallel")),
        name="mamba_conv",
    )(proj, proj, proj, w, b)


def _ssd_kernel(xbc_ref, dt_ref, dtb_ref, alog_ref, hexp_ref, y_ref, h_ref):
    d = pl.program_id(1)
    c = pl.program_id(2)

    @pl.when(c == 0)
    def _():
        h_ref[...] = jnp.zeros_like(h_ref)

    ln = SSD_CHUNK
    row = lax.broadcasted_iota(jnp.int32, (ln, ln), 0)
    col = lax.broadcasted_iota(jnp.int32, (ln, ln), 1)
    incl = (row - col) * (1 - 2 * d) >= 0
    m_incl = jnp.where(incl, 1.0, 0.0).astype(BF16)

    dt = _softplus(dt_ref[0] + dtb_ref[0])
    da = dt * (-jnp.exp(alog_ref[0]))
    cum = _sel_left(m_incl, da)
    tot = jnp.sum(da, axis=0, keepdims=True)
    cum_t = cum.T
    hexp = hexp_ref[...]
    dt_x = _dot(dt.astype(BF16), hexp)
    dst_x = _dot((jnp.exp(tot - cum) * dt).astype(BF16), hexp)
    ecum_x = _dot(jnp.exp(cum).astype(BF16), hexp)
    cdec = jnp.exp(jnp.broadcast_to(tot, (8, LANES)))
    cdec_x = _sel_right(cdec, hexp, pieces=3)[0:1]

    x = xbc_ref[0, :, 0:M_INNER].astype(F32)
    xdt = (x * dt_x).astype(BF16)
    xdst = (x * dst_x).astype(BF16)

    lane = lax.broadcasted_iota(jnp.int32, (1, LANES), 1)
    left = lane < M_HEADDIM
    eye_l = jnp.where(row == col, 1.0, 0.0).astype(BF16)
    gw = M_INNER // M_GROUPS
    hpg = M_HEADS // M_GROUPS
    b_off = M_INNER
    c_off = M_INNER + M_GROUPS * M_STATE
    groups = range(M_GROUPS)
    gsl = lambda g: slice(g * gw, (g + 1) * gw)
    bg = {g: xbc_ref[0, :, b_off + g * M_STATE:b_off + (g + 1) * M_STATE] for g in groups}
    cg = {g: xbc_ref[0, :, c_off + g * M_STATE:c_off + (g + 1) * M_STATE] for g in groups}
    cb = {g: _dot_nt(cg[g], bg[g]) for g in groups}
    hg = {g: h_ref[g] for g in groups}
    y_in = {g: _dot(cg[g], hg[g].astype(BF16)) * ecum_x[:, gsl(g)] for g in groups}
    bg_t = {g: _dot_nt(eye_l, bg[g]).astype(BF16) for g in groups}
    for g in groups:
        h_ref[g] = hg[g] * cdec_x[:, gsl(g)] + _dot(bg_t[g], xdst[:, gsl(g)])
    for g in groups:
        for pr in range(hpg // 2):
            lo = g * gw + pr * LANES
            ys = []
            for hh in range(2):
                e = g * hpg + 2 * pr + hh
                seg = cum[:, e:e + 1] - cum_t[e:e + 1, :]
                decay = jnp.exp(jnp.where(incl, seg, -1e30))
                ys.append(_dot((cb[g] * decay).astype(BF16), xdt[:, lo:lo + LANES]))
            y = jnp.where(left, ys[0], ys[1]) + y_in[g][:, pr * LANES:(pr + 1) * LANES]
            y_ref[0, 0, :, lo:lo + LANES] = y.astype(y_ref.dtype)


def _ssd(xbc_act, dt_raw, dt_bias, a_log, hexp):
    n, t, c = xbc_act.shape
    ln = SSD_CHUNK
    nc = t // ln

    def tmap(b, d, ci):
        return (b, ci + d * (nc - 1 - 2 * ci), 0)

    def dmap(b, d, ci):
        return (d, b, ci + d * (nc - 1 - 2 * ci), 0)

    par = pl.BlockSpec((1, 1, LANES), lambda b, d, ci: (d, 0, 0))
    return pl.pallas_call(
        _ssd_kernel,
        grid=(n, 2, nc),
        in_specs=[pl.BlockSpec((1, ln, c), tmap), pl.BlockSpec((1, ln, LANES), tmap), par, par,
                  _resident(hexp.shape)],
        out_specs=pl.BlockSpec((1, 1, ln, M_INNER), dmap),
        out_shape=jax.ShapeDtypeStruct((2, n, t, M_INNER), BF16),
        scratch_shapes=[pltpu.VMEM((M_GROUPS, M_STATE, M_INNER // M_GROUPS), F32)],
        compiler_params=_cparams(("parallel", "arbitrary", "arbitrary")),
        name="ssd_scan",
    )(xbc_act, dt_raw, dt_bias, a_log, hexp)


def _merge_kernel(yw_ref, bonus_ref, g_ref, ym_ref, xs_ref, z_ref, gates_ref, x_ref,
                  hdown_ref, hup_ref, gng_ref, gnb_ref, dskip_ref, ng_ref, wbr_ref, wbm_ref, wo_ref,
                  lng_ref, lnb_ref, o_ref):
    y = yw_ref[0, 0].astype(F32) + yw_ref[1, 0].astype(F32)
    hdown = hdown_ref[...]
    hup = hup_ref[...]
    mu = _head_sum(y, hdown, hup) * (1.0 / RW_HEAD)
    yc = y - mu
    var = _head_sum(yc * yc, hdown, hup) * (1.0 / RW_HEAD)
    yn = yc * lax.rsqrt(var + GN_EPS) * gng_ref[...] + gnb_ref[...]
    rw = (yn + bonus_ref[0].astype(F32)) * g_ref[0].astype(F32)
    z = z_ref[0].astype(F32)
    ym = ym_ref[0, 0].astype(F32) + ym_ref[1, 0].astype(F32) + dskip_ref[...] * xs_ref[0].astype(F32)
    ym = ym * (z * _sigmoid(z))
    gw = M_INNER // M_GROUPS
    parts = []
    for gi in range(M_GROUPS):
        yg = ym[:, gi * gw:(gi + 1) * gw]
        ms = jnp.mean(yg * yg, axis=-1, keepdims=True)
        parts.append(yg * lax.rsqrt(ms + LN_EPS))
    mo = jnp.concatenate(parts, axis=1) * ng_ref[...]
    u_rw = _dot(rw.astype(BF16), wbr_ref[...])
    u_m = _dot(mo.astype(BF16), wbm_ref[...])
    gates = gates_ref[0]
    mixed = gates[:, :D_MODEL].astype(F32) * u_rw + gates[:, D_MODEL:].astype(F32) * u_m
    mix = _dot(mixed.astype(BF16), wo_ref[...])
    o_ref[0] = _layer_norm(ALPHA * x_ref[0] + mix, lng_ref[...], lnb_ref[...])


def _merge(yw, bonus, g, ym, xbc_act, proj, x, hdown, hup, gn_g, gn_b, dskip, norm_g, w_br, w_bm, w_o,
           ln_g, ln_b, tm=256):
    n, t, _ = x.shape
    tm = min(tm, t)
    tile = lambda width: pl.BlockSpec((1, tm, width), lambda b, i: (b, i, 0))
    dtile = lambda width: pl.BlockSpec((2, 1, tm, width), lambda b, i: (0, b, i, 0))
    res = [hdown, hup, gn_g, gn_b, dskip, norm_g, w_br, w_bm, w_o, ln_g, ln_b]
    return pl.pallas_call(
        _merge_kernel,
        grid=(n, t // tm),
        in_specs=[dtile(RW_WIDTH), tile(RW_WIDTH), tile(RW_WIDTH), dtile(M_INNER), tile(M_INNER),
                  pl.BlockSpec((1, tm, M_INNER), lambda b, i: (b, i, COL_Z // M_INNER)),
                  pl.BlockSpec((1, tm, 2 * D_MODEL), lambda b, i: (b, i, COL_GATES // (2 * D_MODEL))),
                  tile(D_MODEL)] + [_resident(a.shape) for a in res],
        out_specs=tile(D_MODEL),
        out_shape=jax.ShapeDtypeStruct(x.shape, F32),
        compiler_params=_cparams(("parallel", "parallel")),
        name="merge_ln1",
    )(yw, bonus, g, ym, xbc_act, proj, proj, x, *res)


def _attn_kernel(x_ref, kv_ref, wq_ref, wco_ref, lng_ref, lnb_ref, o_ref):
    x = x_ref[0]
    q = _dot(x.astype(BF16), wq_ref[...]).astype(BF16)
    scale = 1.0 / math.sqrt(X_HEAD_DIM)
    out = None
    for h in range(X_HEADS):
        sl = slice(h * X_HEAD_DIM, (h + 1) * X_HEAD_DIM)
        s = _dot_nt(q[:, sl], kv_ref[0, :, sl]) * scale
        s = s - jnp.max(s, axis=-1, keepdims=True)
        e = jnp.exp(s)
        pr = e / jnp.sum(e, axis=-1, keepdims=True)
        o = _dot(pr.astype(BF16), kv_ref[0, :, D_MODEL + h * X_HEAD_DIM:D_MODEL + (h + 1) * X_HEAD_DIM])
        c = _dot(o.astype(BF16), wco_ref[sl, :])
        out = c if out is None else out + c
    o_ref[0] = _layer_norm(ALPHA * x + out, lng_ref[...], lnb_ref[...])


def _attention(x, kv, w_q, w_co, ln_g, ln_b, tm=512):
    n, t, _ = x.shape
    tm = min(tm, t)
    tile = pl.BlockSpec((1, tm, D_MODEL), lambda b, i: (b, i, 0))
    return pl.pallas_call(
        _attn_kernel,
        grid=(n, t // tm),
        in_specs=[tile, pl.BlockSpec((1, N_MEM, 2 * D_MODEL), lambda b, i: (b, 0, 0)),
                  _resident(w_q.shape), _resident(w_co.shape), _resident(ln_g.shape), _resident(ln_b.shape)],
        out_specs=tile,
        out_shape=jax.ShapeDtypeStruct(x.shape, F32),
        compiler_params=_cparams(("parallel", "parallel")),
        name="mem_attention_ln2",
    )(x, kv, w_q, w_co, ln_g, ln_b)


def _mlp_kernel(x_ref, wu_ref, wd_ref, lng_ref, lnb_ref, o_ref, acc_ref, xb_ref):
    f = pl.program_id(1)

    @pl.when(f == 0)
    def _():
        xb_ref[...] = x_ref[...].astype(BF16)
        acc_ref[...] = jnp.zeros_like(acc_ref)

    h = jnp.maximum(_dot(xb_ref[...], wu_ref[...]), 0.0)
    acc_ref[...] += _dot((h * h).astype(BF16), wd_ref[...])

    @pl.when(f == pl.num_programs(1) - 1)
    def _():
        o_ref[...] = _layer_norm(ALPHA * x_ref[...] + acc_ref[...], lng_ref[...], lnb_ref[...])


def _mlp(x, w_up, w_down, ln_g, ln_b, tm=512, tf=1024):
    m, dm = x.shape
    tm = min(tm, m)
    return pl.pallas_call(
        _mlp_kernel,
        grid=(m // tm, D_FF // tf),
        in_specs=[pl.BlockSpec((tm, dm), lambda i, f: (i, 0)),
                  pl.BlockSpec((dm, tf), lambda i, f: (0, f)),
                  pl.BlockSpec((tf, dm), lambda i, f: (f, 0)),
                  _resident(ln_g.shape), _resident(ln_b.shape)],
        out_specs=pl.BlockSpec((tm, dm), lambda i, f: (i, 0)),
        out_shape=jax.ShapeDtypeStruct((m, dm), F32),
        scratch_shapes=[pltpu.VMEM((tm, dm), F32), pltpu.VMEM((tm, dm), BF16)],
        compiler_params=_cparams(("parallel", "arbitrary")),
        name="mlp_ln3",
    )(x, w_up, w_down, ln_g, ln_b)


def _pad_cols(a, width):
    return jnp.pad(a, [(0, 0)] * (a.ndim - 1) + [(0, width - a.shape[-1])])


def _rw_cols(a):
    w3 = 3 * RW_WIDTH
    return jnp.concatenate([a[..., :w3],
                            _pad_cols(a[..., w3:w3 + DECAY_LORA], LORA_PAD),
                            _pad_cols(a[..., w3 + DECAY_LORA:w3 + DECAY_LORA + ICLR_LORA], LORA_PAD),
                            a[..., w3 + DECAY_LORA + ICLR_LORA:RW_SHIFT_COLS]], axis=-1)


def _prepare(w_in, rw_mu_prev, rw_mu_next, rw_w0, rw_w2, rw_a0, rw_a2, rw_g2, rw_k_k, rw_k_a, rw_r_k,
             rw_gn_g, rw_gn_b, m_conv_w, m_conv_b, m_dt_bias, m_a_log, m_d, m_norm_g, w_br, w_bm, w_o,
             ln1_g, ln1_b, w_q, w_kv, w_co, ln2_g, ln2_b, w_up, w_down, ln3_g, ln3_b):
    l = 0
    w = w_in[l]
    c0 = RW_SHIFT_COLS
    c1 = c0 + M_INNER
    c2 = c1 + M_CONV_CH
    c3 = c2 + M_HEADS
    row = lambda a: a.reshape(1, -1).astype(F32)
    head = jnp.arange(RW_WIDTH) // RW_HEAD
    lane_head = jnp.arange(M_INNER) // M_HEADDIM
    return dict(
        w_tiles=jnp.concatenate([_pad_cols(_rw_cols(w[:, :c0]), COL_XBC), w[:, c1:c2], w[:, c3:], w[:, c0:c1]],
                                axis=1).astype(BF16).reshape(D_MODEL, INPROJ_COLS // INPROJ_TN, INPROJ_TN
                                                             ).transpose(1, 0, 2),
        w_dt=_pad_cols(w[:, c2:c3], LANES).astype(BF16),
        mup=row(_rw_cols(rw_mu_prev[l])), mun=row(_rw_cols(rw_mu_next[l])),
        w0=rw_w0[l].astype(F32), a0=rw_a0[l].astype(F32),
        w2=jnp.pad(rw_w2[l], ((0, LORA_PAD - DECAY_LORA), (0, 0))).astype(BF16),
        a2=jnp.pad(rw_a2[l], ((0, LORA_PAD - ICLR_LORA), (0, 0))).astype(BF16),
        g2=rw_g2[l].astype(BF16),
        k_k=row(rw_k_k[l]), k_a=row(rw_k_a[l]), r_k=row(rw_r_k[l]),
        gn_g=row(rw_gn_g[l]), gn_b=row(rw_gn_b[l]),
        hdown=(head[:, None] == jnp.arange(LANES)[None, :]).astype(BF16),
        hup=(jnp.arange(LANES)[:, None] == head[None, :]).astype(BF16),
        conv_w=m_conv_w[l].astype(F32), conv_b=row(m_conv_b[l]),
        dt_bias=_pad_cols(m_dt_bias[l], LANES).reshape(2, 1, LANES).astype(F32),
        a_log=_pad_cols(m_a_log[l], LANES).reshape(2, 1, LANES).astype(F32),
        hexp=(jnp.arange(LANES)[:, None] == lane_head[None, :]).astype(BF16),
        dskip=row(jnp.repeat(m_d[l], M_HEADDIM)), norm_g=row(m_norm_g[l]),
        w_br=w_br[l].astype(BF16), w_bm=w_bm[l].astype(BF16), w_o=w_o[l].astype(BF16),
        ln1_g=row(ln1_g[l]), ln1_b=row(ln1_b[l]),
        w_q=w_q[l].astype(BF16), w_kv=w_kv[l].astype(BF16), w_co=w_co[l].astype(BF16),
        ln2_g=row(ln2_g[l]), ln2_b=row(ln2_b[l]),
        w_up=w_up[l].astype(BF16), w_down=w_down[l].astype(BF16),
        ln3_g=row(ln3_g[l]), ln3_b=row(ln3_b[l]),
    )


def _encoder_layer(x, mem, p):
    n, t, dm = x.shape
    x2 = x.reshape(n * t, dm)
    as3 = lambda a: a.reshape(n, t, a.shape[-1])
    proj, dt_raw = _inproj(x2, p["w_tiles"], p["w_dt"])
    proj = as3(proj)
    dt_raw = as3(dt_raw)

    r, k, v, kn, g, bonus, ld, a = _rw_prep(proj, p["mup"], p["mun"], p["w2"], p["a2"], p["g2"],
                                            p["w0"], p["a0"], p["k_k"], p["r_k"], p["hdown"], p["hup"])
    yw = _wkv(r, k, v, kn, ld, a, p["k_a"])

    xbc_act = _conv_silu(proj, p["conv_w"], p["conv_b"])
    ym = _ssd(xbc_act, dt_raw, p["dt_bias"], p["a_log"], p["hexp"])

    x1 = _merge(yw, bonus, g, ym, xbc_act, proj, x, p["hdown"], p["hup"], p["gn_g"], p["gn_b"], p["dskip"],
                p["norm_g"], p["w_br"], p["w_bm"], p["w_o"], p["ln1_g"], p["ln1_b"])

    kv = _mm(mem.reshape(n * N_MEM, dm), p["w_kv"]).reshape(n, N_MEM, 2 * dm)
    x2_ = _attention(x1, kv, p["w_q"], p["w_co"], p["ln2_g"], p["ln2_b"])
    out = _mlp(x2_.reshape(n * t, dm), p["w_up"], p["w_down"], p["ln3_g"], p["ln3_b"])
    return out.reshape(n, t, dm)


def kernel(x_prompt, x_sample, mem_prompt, mem_sample, w_in, rw_mu_prev, rw_mu_next, rw_w0, rw_w2, rw_a0, rw_a2, rw_g2, rw_k_k, rw_k_a, rw_r_k, rw_gn_g, rw_gn_b, m_conv_w, m_conv_b, m_dt_bias, m_a_log, m_d, m_norm_g, w_br, w_bm, w_o, ln1_g, ln1_b, w_q, w_kv, w_co, ln2_g, ln2_b, w_up, w_down, ln3_g, ln3_b):
    p = _prepare(w_in, rw_mu_prev, rw_mu_next, rw_w0, rw_w2, rw_a0, rw_a2, rw_g2, rw_k_k, rw_k_a, rw_r_k,
                 rw_gn_g, rw_gn_b, m_conv_w, m_conv_b, m_dt_bias, m_a_log, m_d, m_norm_g, w_br, w_bm, w_o,
                 ln1_g, ln1_b, w_q, w_kv, w_co, ln2_g, ln2_b, w_up, w_down, ln3_g, ln3_b)
    return (_encoder_layer(x_prompt, mem_prompt, p), _encoder_layer(x_sample, mem_sample, p))
```

```python
import functools
import math

import jax
import jax.numpy as jnp
from jax import lax
from jax.experimental import pallas as pl
from jax.experimental.pallas import tpu as pltpu

F32 = jnp.float32
BF16 = jnp.bfloat16

D_MODEL = 2048
RW_HEAD = 64
RW_WIDTH = D_MODEL // 2
RW_HEADS = RW_WIDTH // RW_HEAD
DECAY_LORA = 96
ICLR_LORA = 96
GATE_LORA = 256
LORA_PAD = 128
M_INNER = D_MODEL
M_HEADDIM = 64
M_HEADS = M_INNER // M_HEADDIM
M_STATE = 128
M_GROUPS = 8
CONV_W = 5
N_MEM = 256
X_HEADS = 4
X_HEAD_DIM = D_MODEL // X_HEADS
D_FF = 4 * D_MODEL
DEPTH = 1
ALPHA = (2.0 * DEPTH) ** 0.25
LN_EPS = 1e-5
GN_EPS = 64e-5

RW_SHIFT_COLS = 3 * RW_WIDTH + DECAY_LORA + ICLR_LORA + GATE_LORA
RW_PAD_COLS = 3 * RW_WIDTH + 2 * LORA_PAD + GATE_LORA
M_CONV_CH = M_INNER + 2 * M_GROUPS * M_STATE

WKV_CHUNK = 64
SSD_CHUNK = 128
HALO = 16
LANES = 128
VMEM_LIMIT = 56 * 1024 * 1024


def _cparams(sem):
    return pltpu.CompilerParams(dimension_semantics=sem, vmem_limit_bytes=VMEM_LIMIT)


def _resident(shape):
    nd = len(shape)
    return pl.BlockSpec(shape, lambda *_: (0,) * nd, pipeline_mode=pl.Buffered(1))


def _split3(a):
    a1 = a.astype(BF16)
    r1 = a - a1.astype(F32)
    a2 = r1.astype(BF16)
    a3 = (r1 - a2.astype(F32)).astype(BF16)
    return a1, a2, a3


def _dot(a, b):
    return jnp.dot(a, b, preferred_element_type=F32)


def _dot_nt(a, b):
    return lax.dot_general(a, b, (((1,), (1,)), ((), ())), preferred_element_type=F32)


def _sel_left(m01, a):
    a1, a2, a3 = _split3(a)
    return _dot(m01, a1) + _dot(m01, a2) + _dot(m01, a3)


def _sel_right(a, m01, pieces=2):
    a1, a2, a3 = _split3(a)
    out = _dot(a1, m01) + _dot(a2, m01)
    if pieces == 3:
        out = out + _dot(a3, m01)
    return out


def _head_sum(a, down, up):
    return _sel_right(_sel_right(a, down), up)


def _softplus(x):
    return jnp.maximum(x, 0.0) + jnp.log1p(jnp.exp(-jnp.abs(x)))


def _sigmoid(x):
    return 1.0 / (1.0 + jnp.exp(-x))


def _layer_norm(x, g, b):
    mu = jnp.mean(x, axis=-1, keepdims=True)
    xc = x - mu
    var = jnp.mean(xc * xc, axis=-1, keepdims=True)
    return xc * lax.rsqrt(var + LN_EPS) * g + b


def _mm_kernel(x_ref, w_ref, o_ref, xb_ref):
    @pl.when(pl.program_id(1) == 0)
    def _():
        xb_ref[...] = x_ref[...].astype(BF16)

    o_ref[...] = _dot(xb_ref[...], w_ref[...]).astype(o_ref.dtype)


def _mm(x, w, *, out_dtype=BF16, tm=1024, tn=512):
    m, k = x.shape
    n = w.shape[1]
    tm = min(tm, m)
    tn = min(tn, n)
    return pl.pallas_call(
        _mm_kernel,
        grid=(m // tm, n // tn),
        in_specs=[pl.BlockSpec((tm, k), lambda i, j: (i, 0)),
                  pl.BlockSpec((k, tn), lambda i, j: (0, j))],
        out_specs=pl.BlockSpec((tm, tn), lambda i, j: (i, j)),
        out_shape=jax.ShapeDtypeStruct((m, n), out_dtype),
        scratch_shapes=[pltpu.VMEM((tm, k), BF16)],
        compiler_params=_cparams(("parallel", "arbitrary")),
        name="matmul",
    )(x, w)


INPROJ_TN = 1024
COL_RW = 0
COL_XBC = 4 * INPROJ_TN
COL_GATES = COL_XBC + M_CONV_CH
COL_Z = COL_GATES + 2 * D_MODEL
INPROJ_COLS = COL_Z + M_INNER


def _inproj_kernel(x_ref, w_ref, wdt_ref, o_ref, dt_ref, xb_ref):
    j = pl.program_id(1)

    @pl.when(j == 0)
    def _():
        xb = x_ref[...].astype(BF16)
        xb_ref[...] = xb
        dt_ref[...] = _dot(xb, wdt_ref[...])

    acc = _dot(xb_ref[...], w_ref[0])
    is_gate = jnp.logical_and(j >= COL_GATES // INPROJ_TN, j < COL_Z // INPROJ_TN)

    @pl.when(is_gate)
    def _():
        o_ref[...] = _sigmoid(acc).astype(o_ref.dtype)

    @pl.when(jnp.logical_not(is_gate))
    def _():
        o_ref[...] = acc.astype(o_ref.dtype)


def _inproj(x, w_tiles, w_dt, tm=1024):
    m, k = x.shape
    tm = min(tm, m)
    tn = INPROJ_TN
    return pl.pallas_call(
        _inproj_kernel,
        grid=(m // tm, INPROJ_COLS // tn),
        in_specs=[pl.BlockSpec((tm, k), lambda i, j: (i, 0)),
                  pl.BlockSpec((1, k, tn), lambda i, j: (j, 0, 0)),
                  _resident(w_dt.shape)],
        out_specs=[pl.BlockSpec((tm, tn), lambda i, j: (i, j)),
                   pl.BlockSpec((tm, LANES), lambda i, j: (i, 0))],
        out_shape=[jax.ShapeDtypeStruct((m, INPROJ_COLS), BF16), jax.ShapeDtypeStruct((m, LANES), F32)],
        scratch_shapes=[pltpu.VMEM((tm, k), BF16)],
        compiler_params=_cparams(("parallel", "arbitrary")),
        name="in_projection",
    )(x, w_tiles, w_dt)


def _halo_specs(t, tt, width):
    per = tt // HALO
    last = t // HALO - 1

    def prev_map(b, i):
        return (b, jnp.maximum(i * per - 1, 0), 0)

    def next_map(b, i):
        return (b, jnp.minimum((i + 1) * per, last), 0)

    return (pl.BlockSpec((1, HALO, width), prev_map), pl.BlockSpec((1, HALO, width), next_map))


SHIFT_ROWS = 128
SHIFT_LANES = 256


def _shift_matrix(rb, off):
    row = lax.broadcasted_iota(jnp.int32, (rb, rb + 2 * HALO), 0)
    col = lax.broadcasted_iota(jnp.int32, (rb, rb + 2 * HALO), 1)
    return jnp.where(col - row == HALO + off, 1.0, 0.0).astype(BF16)


def _extended_block(x_ref, prev_rows, next_rows, r0, cl):
    tt = x_ref.shape[1]
    rb = SHIFT_ROWS
    above = prev_rows[:, cl] if r0 == 0 else x_ref[0, r0 - HALO:r0, cl]
    below = next_rows[:, cl] if r0 + rb == tt else x_ref[0, r0 + rb:r0 + rb + HALO, cl]
    xb = x_ref[0, r0:r0 + rb, cl]
    return xb, jnp.concatenate([above, xb, below], axis=0)


def _rw_prep_kernel(p_ref, pp_ref, pn_ref, mup_ref, mun_ref, w2_ref, a2_ref, g2_ref, w0_ref, a0_ref,
                    kk_ref, rk_ref, hdown_ref, hup_ref,
                    r_out, k_out, v_out, kn_out, g_out, bonus_out, ld_out, a_out, ps_ref):
    i = pl.program_id(1)
    nt = pl.num_programs(1)
    tt, c = p_ref.shape[1], p_ref.shape[2]
    prev_rows = jnp.where(i > 0, pp_ref[0], jnp.zeros_like(pp_ref[0]))
    next_rows = jnp.where(i < nt - 1, pn_ref[0], jnp.zeros_like(pn_ref[0]))
    s_prev = _shift_matrix(SHIFT_ROWS, -1)
    s_next = _shift_matrix(SHIFT_ROWS, 1)
    for r0 in range(0, tt, SHIFT_ROWS):
        for c0 in range(0, c, SHIFT_LANES):
            cl = slice(c0, c0 + SHIFT_LANES)
            pb, pe = _extended_block(p_ref, prev_rows, next_rows, r0, cl)
            p = pb.astype(F32)
            ps_ref[r0:r0 + SHIFT_ROWS, cl] = (p + mup_ref[:, cl] * (_dot(s_prev, pe) - p)
                                               + mun_ref[:, cl] * (_dot(s_next, pe) - p))
    w = RW_WIDTH
    r = ps_ref[:, 0:w]
    k = ps_ref[:, w:2 * w]
    v = ps_ref[:, 2 * w:3 * w]
    dw = ps_ref[:, 3 * w:3 * w + LORA_PAD]
    da = ps_ref[:, 3 * w + LORA_PAD:3 * w + 2 * LORA_PAD]
    dg = ps_ref[:, 3 * w + 2 * LORA_PAD:]
    hw = _dot(jnp.tanh(dw).astype(BF16), w2_ref[...])
    ha = _dot(da.astype(BF16), a2_ref[...])
    g = _dot(_sigmoid(dg).astype(BF16), g2_ref[...])
    for d in range(2):
        ld_out[d, 0] = -math.exp(-0.5) * _sigmoid(w0_ref[d:d + 1, :] + hw)
        a_out[d, 0] = _sigmoid(a0_ref[d:d + 1, :] + ha)
    hdown = hdown_ref[...]
    hup = hup_ref[...]
    kkr = k * kk_ref[...]
    ss = _head_sum(kkr * kkr, hdown, hup)
    kn = kkr * lax.rsqrt(jnp.maximum(ss, 1e-24))
    rk = _head_sum(r * k * rk_ref[...], hdown, hup)
    r_out[0] = r.astype(r_out.dtype)
    k_out[0] = k.astype(k_out.dtype)
    v_out[0] = v.astype(v_out.dtype)
    kn_out[0] = kn.astype(kn_out.dtype)
    g_out[0] = g.astype(g_out.dtype)
    bonus_out[0] = (rk * v).astype(bonus_out.dtype)


def _rw_prep(p, mup, mun, w2, a2, g2, w0, a0, k_k, r_k, hdown, hup, tt=256):
    n, t, _ = p.shape
    c = RW_PAD_COLS
    tt = min(tt, t)
    w = RW_WIDTH
    tile = lambda width: pl.BlockSpec((1, tt, width), lambda b, i: (b, i, 0))
    dir_tile = pl.BlockSpec((2, 1, tt, w), lambda b, i: (0, b, i, 0))
    hp, hn = _halo_specs(t, tt, c)
    bf = jax.ShapeDtypeStruct((n, t, w), BF16)
    dd = jax.ShapeDtypeStruct((2, n, t, w), F32)
    return pl.pallas_call(
        _rw_prep_kernel,
        grid=(n, t // tt),
        in_specs=[tile(c), hp, hn, _resident(mup.shape), _resident(mun.shape), _resident(w2.shape),
                  _resident(a2.shape), _resident(g2.shape), _resident(w0.shape), _resident(a0.shape),
                  _resident(k_k.shape), _resident(r_k.shape), _resident(hdown.shape), _resident(hup.shape)],
        out_specs=[tile(w)] * 6 + [dir_tile, dir_tile],
        out_shape=[bf] * 6 + [dd, dd],
        scratch_shapes=[pltpu.VMEM((tt, c), F32)],
        compiler_params=_cparams(("parallel", "parallel")),
        name="rwkv_prep",
    )(p, p, p, mup, mun, w2, a2, g2, w0, a0, k_k, r_k, hdown, hup)


def _wkv_kernel(r_ref, k_ref, v_ref, kn_ref, ld_ref, a_ref, ka_ref, y_ref, h_ref, *, nchunk):
    d = pl.program_id(1)
    c = pl.program_id(2)

    @pl.when(c == 0)
    def _():
        h_ref[...] = jnp.zeros_like(h_ref)

    cs = WKV_CHUNK
    npair = RW_HEADS // 2
    sgn = 1 - 2 * d
    row = lax.broadcasted_iota(jnp.int32, (cs, 2 * cs), 0)
    col = lax.broadcasted_iota(jnp.int32, (cs, 2 * cs), 1) & (cs - 1)
    lag2 = (row - col) * sgn
    strict2 = lag2 > 0
    incl2 = lag2 >= 0
    m_incl = jnp.where(incl2[:, :cs], 1.0, 0.0).astype(BF16)
    eye = jnp.where(lax.broadcasted_iota(jnp.int32, (cs, cs), 0)
                    == lax.broadcasted_iota(jnp.int32, (cs, cs), 1), 1.0, 0.0)
    lane = lax.broadcasted_iota(jnp.int32, (1, LANES), 1)
    left = lane < RW_HEAD
    right = lane >= RW_HEAD
    left2 = (lax.broadcasted_iota(jnp.int32, (cs, 2 * LANES), 1) & (LANES - 1)) < RW_HEAD
    blockdiag = (lax.broadcasted_iota(jnp.int32, (LANES, LANES), 0) // RW_HEAD
                 == lax.broadcasted_iota(jnp.int32, (LANES, LANES), 1) // RW_HEAD)
    eye_l = jnp.where(lax.broadcasted_iota(jnp.int32, (LANES, LANES), 0)
                      == lax.broadcasted_iota(jnp.int32, (LANES, LANES), 1), 1.0, 0.0).astype(BF16)
    zeros_cv = jnp.zeros((cs, LANES), BF16)
    zeros_cf = jnp.zeros((cs, LANES), F32)
    ka = ka_ref[...]

    offs, prep = [], []
    for j in range(nchunk):
        off = pl.multiple_of((j + d * (nchunk - 1 - 2 * j)) * cs, cs)
        offs.append(off)
        ld = ld_ref[0, 0, pl.ds(off, cs), :]
        cum = _sel_left(m_incl, ld)
        tot = jnp.sum(ld, axis=0, keepdims=True)
        r = r_ref[0, pl.ds(off, cs), :].astype(F32)
        k = k_ref[0, pl.ds(off, cs), :].astype(F32)
        kn = kn_ref[0, pl.ds(off, cs), :].astype(F32)
        a = a_ref[0, 0, pl.ds(off, cs), :]
        kdir = k * (1.0 + (a - 1.0) * ka)
        b = kn * a
        e_ncum = jnp.exp(-cum)
        e_rem = jnp.exp(tot - cum)
        prep.append(dict(
            rt=(r * jnp.exp(cum)).astype(BF16), at=(-kn * jnp.exp(cum - ld)).astype(BF16),
            kt=(kdir * e_ncum).astype(BF16), bt=(b * e_ncum).astype(BF16),
            kh=(kdir * e_rem).astype(BF16), bh=(b * e_rem).astype(BF16),
            v=v_ref[0, pl.ds(off, cs), :], wc=jnp.exp(tot)))

    units = [(j, p) for j in range(nchunk) for p in range(npair)]
    heads = [(j, p, hh) for (j, p) in units for hh in range(2)]
    sl = lambda p: slice(p * LANES, (p + 1) * LANES)

    g_up, g_low, a_ab = {}, {}, {}
    for (j, p) in units:
        q = prep[j]
        ar = jnp.concatenate([q["at"][:, sl(p)], q["rt"][:, sl(p)]], axis=0)
        bk = jnp.concatenate([q["bt"][:, sl(p)], q["kt"][:, sl(p)]], axis=0)
        for hh in range(2):
            arm = jnp.where(left if hh == 0 else right, ar, jnp.zeros_like(ar))
            gm = _dot_nt(arm, bk)
            gu = jnp.where(strict2, gm[:cs], 0.0)
            a_ab[j, p, hh] = gu[:, :cs]
            g_up[j, p, hh] = gu.astype(BF16)
            g_low[j, p, hh] = jnp.where(incl2, gm[cs:], 0.0).astype(BF16)

    tinv = {h: a_ab[h] + eye for h in heads}
    pw = dict(a_ab)
    for _ in range(int(math.log2(cs)) - 1):
        for h in heads:
            pb = pw[h].astype(BF16)
            pw[h] = _dot(pb, pb)
        for h in heads:
            tinv[h] = tinv[h] + _dot(tinv[h].astype(BF16), pw[h].astype(BF16))

    w1 = {}
    for (j, p) in units:
        zv = jnp.concatenate([zeros_cv, prep[j]["v"][:, sl(p)]], axis=0)
        for hh in range(2):
            w1[j, p, hh] = _dot(g_up[j, p, hh], zv)
    tx = {}
    for (j, p, hh) in heads:
        rhs = jnp.concatenate([w1[j, p, hh].astype(BF16), prep[j]["at"][:, sl(p)]], axis=1)
        tx[j, p, hh] = _dot(tinv[j, p, hh].astype(BF16), rhs)
    lmat = {}
    for (j, p) in units:
        va = jnp.where(left2, tx[j, p, 0], tx[j, p, 1])
        vrow = jnp.concatenate([prep[j]["v"][:, sl(p)].astype(F32), zeros_cf], axis=1)
        lmat[j, p] = jnp.concatenate([va, vrow], axis=0).astype(BF16)

    z = {h: _dot(g_low[h], lmat[h[0], h[1]]) for h in heads}
    lv_t = {u: _dot_nt(eye_l, lmat[u][:, :LANES]).astype(BF16) for u in units}
    bh_t = {(j, p): _dot_nt(eye_l, prep[j]["bh"][:, sl(p)]).astype(BF16) for (j, p) in units}
    y0, m1, m2, n2t = {}, {}, {}, {}
    for (j, p) in units:
        q = prep[j]
        ym = jnp.where(left2, z[j, p, 0], z[j, p, 1])
        y0[j, p] = ym[:, :LANES]
        m1[j, p] = (q["rt"][:, sl(p)].astype(F32) + ym[:, LANES:]).astype(BF16)
        bkh = jnp.concatenate([q["bh"][:, sl(p)], q["kh"][:, sl(p)]], axis=0)
        n2t[j, p] = jnp.where(blockdiag, _dot(lv_t[j, p], bkh), 0.0)
    for (j, p) in units:
        m2[j, p] = jnp.where(blockdiag, _dot(bh_t[j, p], lmat[j, p][:cs, LANES:]), 0.0).astype(BF16)

    hts = [h_ref[p] for p in range(npair)]
    for j in range(nchunk):
        for p in range(npair):
            htb = hts[p].astype(BF16)
            y = y0[j, p] + _dot_nt(m1[j, p], htb)
            y_ref[0, 0, pl.ds(offs[j], cs), sl(p)] = y.astype(y_ref.dtype)
            hts[p] = hts[p] * prep[j]["wc"][:, sl(p)] + _dot_nt(htb, m2[j, p]) + n2t[j, p]
    for p in range(npair):
        h_ref[p] = hts[p]


def _wkv(r, k, v, kn, ld, a, k_a, nchunk=2):
    n, t, w = r.shape
    tb = WKV_CHUNK * nchunk
    nb = t // tb

    def tmap(b, d, c):
        return (b, c + d * (nb - 1 - 2 * c), 0)

    def dmap(b, d, c):
        return (d, b, c + d * (nb - 1 - 2 * c), 0)

    tile = pl.BlockSpec((1, tb, w), tmap)
    dtile = pl.BlockSpec((1, 1, tb, w), dmap)
    return pl.pallas_call(
        functools.partial(_wkv_kernel, nchunk=nchunk),
        grid=(n, 2, nb),
        in_specs=[tile, tile, tile, tile, dtile, dtile, _resident(k_a.shape)],
        out_specs=dtile,
        out_shape=jax.ShapeDtypeStruct((2, n, t, w), BF16),
        scratch_shapes=[pltpu.VMEM((RW_HEADS // 2, LANES, LANES), F32)],
        compiler_params=_cparams(("parallel", "arbitrary", "arbitrary")),
        name="wkv7_scan",
    )(r, k, v, kn, ld, a, k_a)


def _conv_kernel(x_ref, xp_ref, xn_ref, w_ref, b_ref, o_ref):
    i = pl.program_id(1)
    nt = pl.num_programs(1)
    tt, tc = x_ref.shape[1], x_ref.shape[2]
    half = CONV_W // 2
    rb = SHIFT_ROWS
    shifts = {j: _shift_matrix(rb, j - half) for j in range(CONV_W) if j != half}
    prev_rows = jnp.where(i > 0, xp_ref[0], jnp.zeros_like(xp_ref[0]))
    next_rows = jnp.where(i < nt - 1, xn_ref[0], jnp.zeros_like(xn_ref[0]))
    for r0 in range(0, tt, rb):
        for c0 in range(0, tc, SHIFT_LANES):
            cl = slice(c0, c0 + SHIFT_LANES)
            xb, xe = _extended_block(x_ref, prev_rows, next_rows, r0, cl)
            acc = xb.astype(F32) * w_ref[half:half + 1, cl] + b_ref[:, cl]
            for j, sh in shifts.items():
                acc = acc + _dot(sh, xe) * w_ref[j:j + 1, cl]
            o_ref[0, r0:r0 + rb, cl] = (acc * _sigmoid(acc)).astype(o_ref.dtype)


def _conv_silu(proj, w, b, tt=256, tc=1024):
    n, t, _ = proj.shape
    c = M_CONV_CH
    j0 = COL_XBC // tc
    tt = min(tt, t)
    per = tt // HALO
    nrow = t // HALO
    return pl.pallas_call(
        _conv_kernel,
        grid=(n, t // tt, c // tc),
        in_specs=[pl.BlockSpec((1, tt, tc), lambda b_, i, j: (b_, i, j0 + j)),
                  pl.BlockSpec((1, HALO, tc), lambda b_, i, j: (b_, jnp.maximum(i * per - 1, 0), j0 + j)),
                  pl.BlockSpec((1, HALO, tc), lambda b_, i, j: (b_, jnp.minimum((i + 1) * per, nrow - 1), j0 + j)),
                  pl.BlockSpec((CONV_W, tc), lambda b_, i, j: (0, j)),
                  pl.BlockSpec((1, tc), lambda b_, i, j: (0, j))],
        out_specs=pl.BlockSpec((1, tt, tc), lambda b_, i, j: (b_, i, j)),
        out_shape=jax.ShapeDtypeStruct((n, t, c), BF16),
        compiler_params=_cparams(("parallel", "parallel", "parallel")),
        name="mamba_conv",
    )(proj, proj, proj, w, b)


def _ssd_kernel(xbc_ref, dt_ref, dtb_ref, alog_ref, hexp_ref, y_ref, h_ref):
    d = pl.program_id(1)
    c = pl.program_id(2)

    @pl.when(c == 0)
    def _():
        h_ref[...] = jnp.zeros_like(h_ref)

    ln = SSD_CHUNK
    row = lax.broadcasted_iota(jnp.int32, (ln, ln), 0)
    col = lax.broadcasted_iota(jnp.int32, (ln, ln), 1)
    incl = (row - col) * (1 - 2 * d) >= 0
    m_incl = jnp.where(incl, 1.0, 0.0).astype(BF16)

    dt = _softplus(dt_ref[0] + dtb_ref[0])
    da = dt * (-jnp.exp(alog_ref[0]))
    cum = _sel_left(m_incl, da)
    tot = jnp.sum(da, axis=0, keepdims=True)
    cum_t = cum.T
    hexp = hexp_ref[...]
    dt_x = _dot(dt.astype(BF16), hexp)
    dst_x = _dot((jnp.exp(tot - cum) * dt).astype(BF16), hexp)
    ecum_x = _dot(jnp.exp(cum).astype(BF16), hexp)
    cdec = jnp.exp(jnp.broadcast_to(tot, (8, LANES)))
    cdec_x = _sel_right(cdec, hexp, pieces=3)[0:1]

    x = xbc_ref[0, :, 0:M_INNER].astype(F32)
    xdt = (x * dt_x).astype(BF16)
    xdst = (x * dst_x).astype(BF16)

    lane = lax.broadcasted_iota(jnp.int32, (1, LANES), 1)
    left = lane < M_HEADDIM
    eye_l = jnp.where(row == col, 1.0, 0.0).astype(BF16)
    gw = M_INNER // M_GROUPS
    hpg = M_HEADS // M_GROUPS
    b_off = M_INNER
    c_off = M_INNER + M_GROUPS * M_STATE
    groups = range(M_GROUPS)
    gsl = lambda g: slice(g * gw, (g + 1) * gw)
    bg = {g: xbc_ref[0, :, b_off + g * M_STATE:b_off + (g + 1) * M_STATE] for g in groups}
    cg = {g: xbc_ref[0, :, c_off + g * M_STATE:c_off + (g + 1) * M_STATE] for g in groups}
    cb = {g: _dot_nt(cg[g], bg[g]) for g in groups}
    hg = {g: h_ref[g] for g in groups}
    y_in = {g: _dot(cg[g], hg[g].astype(BF16)) * ecum_x[:, gsl(g)] for g in groups}
    bg_t = {g: _dot_nt(eye_l, bg[g]).astype(BF16) for g in groups}
    for g in groups:
        h_ref[g] = hg[g] * cdec_x[:, gsl(g)] + _dot(bg_t[g], xdst[:, gsl(g)])
    for g in groups:
        for pr in range(hpg // 2):
            lo = g * gw + pr * LANES
            ys = []
            for hh in range(2):
                e = g * hpg + 2 * pr + hh
                seg = cum[:, e:e + 1] - cum_t[e:e + 1, :]
                decay = jnp.exp(jnp.where(incl, seg, -1e30))
                ys.append(_dot((cb[g] * decay).astype(BF16), xdt[:, lo:lo + LANES]))
            y = jnp.where(left, ys[0], ys[1]) + y_in[g][:, pr * LANES:(pr + 1) * LANES]
            y_ref[0, 0, :, lo:lo + LANES] = y.astype(y_ref.dtype)


def _ssd(xbc_act, dt_raw, dt_bias, a_log, hexp):
    n, t, c = xbc_act.shape
    ln = SSD_CHUNK
    nc = t // ln

    def tmap(b, d, ci):
        return (b, ci + d * (nc - 1 - 2 * ci), 0)

    def dmap(b, d, ci):
        return (d, b, ci + d * (nc - 1 - 2 * ci), 0)

    par = pl.BlockSpec((1, 1, LANES), lambda b, d, ci: (d, 0, 0))
    return pl.pallas_call(
        _ssd_kernel,
        grid=(n, 2, nc),
        in_specs=[pl.BlockSpec((1, ln, c), tmap), pl.BlockSpec((1, ln, LANES), tmap), par, par,
                  _resident(hexp.shape)],
        out_specs=pl.BlockSpec((1, 1, ln, M_INNER), dmap),
        out_shape=jax.ShapeDtypeStruct((2, n, t, M_INNER), BF16),
        scratch_shapes=[pltpu.VMEM((M_GROUPS, M_STATE, M_INNER // M_GROUPS), F32)],
        compiler_params=_cparams(("parallel", "arbitrary", "arbitrary")),
        name="ssd_scan",
    )(xbc_act, dt_raw, dt_bias, a_log, hexp)


def _merge_kernel(yw_ref, bonus_ref, g_ref, ym_ref, xs_ref, z_ref, gates_ref, x_ref,
                  hdown_ref, hup_ref, gng_ref, gnb_ref, dskip_ref, ng_ref, wbr_ref, wbm_ref, wo_ref,
                  lng_ref, lnb_ref, o_ref):
    y = yw_ref[0, 0].astype(F32) + yw_ref[1, 0].astype(F32)
    hdown = hdown_ref[...]
    hup = hup_ref[...]
    mu = _head_sum(y, hdown, hup) * (1.0 / RW_HEAD)
    yc = y - mu
    var = _head_sum(yc * yc, hdown, hup) * (1.0 / RW_HEAD)
    yn = yc * lax.rsqrt(var + GN_EPS) * gng_ref[...] + gnb_ref[...]
    rw = (yn + bonus_ref[0].astype(F32)) * g_ref[0].astype(F32)
    z = z_ref[0].astype(F32)
    ym = ym_ref[0, 0].astype(F32) + ym_ref[1, 0].astype(F32) + dskip_ref[...] * xs_ref[0].astype(F32)
    ym = ym * (z * _sigmoid(z))
    gw = M_INNER // M_GROUPS
    parts = []
    for gi in range(M_GROUPS):
        yg = ym[:, gi * gw:(gi + 1) * gw]
        ms = jnp.mean(yg * yg, axis=-1, keepdims=True)
        parts.append(yg * lax.rsqrt(ms + LN_EPS))
    mo = jnp.concatenate(parts, axis=1) * ng_ref[...]
    u_rw = _dot(rw.astype(BF16), wbr_ref[...])
    u_m = _dot(mo.astype(BF16), wbm_ref[...])
    gates = gates_ref[0]
    mixed = gates[:, :D_MODEL].astype(F32) * u_rw + gates[:, D_MODEL:].astype(F32) * u_m
    mix = _dot(mixed.astype(BF16), wo_ref[...])
    o_ref[0] = _layer_norm(ALPHA * x_ref[0] + mix, lng_ref[...], lnb_ref[...])


def _merge(yw, bonus, g, ym, xbc_act, proj, x, hdown, hup, gn_g, gn_b, dskip, norm_g, w_br, w_bm, w_o,
           ln_g, ln_b, tm=256):
    n, t, _ = x.shape
    tm = min(tm, t)
    tile = lambda width: pl.BlockSpec((1, tm, width), lambda b, i: (b, i, 0))
    dtile = lambda width: pl.BlockSpec((2, 1, tm, width), lambda b, i: (0, b, i, 0))
    res = [hdown, hup, gn_g, gn_b, dskip, norm_g, w_br, w_bm, w_o, ln_g, ln_b]
    return pl.pallas_call(
        _merge_kernel,
        grid=(n, t // tm),
        in_specs=[dtile(RW_WIDTH), tile(RW_WIDTH), tile(RW_WIDTH), dtile(M_INNER), tile(M_INNER),
                  pl.BlockSpec((1, tm, M_INNER), lambda b, i: (b, i, COL_Z // M_INNER)),
                  pl.BlockSpec((1, tm, 2 * D_MODEL), lambda b, i: (b, i, COL_GATES // (2 * D_MODEL))),
                  tile(D_MODEL)] + [_resident(a.shape) for a in res],
        out_specs=tile(D_MODEL),
        out_shape=jax.ShapeDtypeStruct(x.shape, F32),
        compiler_params=_cparams(("parallel", "parallel")),
        name="merge_ln1",
    )(yw, bonus, g, ym, xbc_act, proj, proj, x, *res)


def _attn_kernel(x_ref, kv_ref, wq_ref, wco_ref, lng_ref, lnb_ref, o_ref):
    x = x_ref[0]
    q = _dot(x.astype(BF16), wq_ref[...]).astype(BF16)
    scale = 1.0 / math.sqrt(X_HEAD_DIM)
    out = None
    for h in range(X_HEADS):
        sl = slice(h * X_HEAD_DIM, (h + 1) * X_HEAD_DIM)
        s = _dot_nt(q[:, sl], kv_ref[0, :, sl]) * scale
        s = s - jnp.max(s, axis=-1, keepdims=True)
        e = jnp.exp(s)
        pr = e / jnp.sum(e, axis=-1, keepdims=True)
        o = _dot(pr.astype(BF16), kv_ref[0, :, D_MODEL + h * X_HEAD_DIM:D_MODEL + (h + 1) * X_HEAD_DIM])
        c = _dot(o.astype(BF16), wco_ref[sl, :])
        out = c if out is None else out + c
    o_ref[0] = _layer_norm(ALPHA * x + out, lng_ref[...], lnb_ref[...])


def _attention(x, kv, w_q, w_co, ln_g, ln_b, tm=512):
    n, t, _ = x.shape
    tm = min(tm, t)
    tile = pl.BlockSpec((1, tm, D_MODEL), lambda b, i: (b, i, 0))
    return pl.pallas_call(
        _attn_kernel,
        grid=(n, t // tm),
        in_specs=[tile, pl.BlockSpec((1, N_MEM, 2 * D_MODEL), lambda b, i: (b, 0, 0)),
                  _resident(w_q.shape), _resident(w_co.shape), _resident(ln_g.shape), _resident(ln_b.shape)],
        out_specs=tile,
        out_shape=jax.ShapeDtypeStruct(x.shape, F32),
        compiler_params=_cparams(("parallel", "parallel")),
        name="mem_attention_ln2",
    )(x, kv, w_q, w_co, ln_g, ln_b)


def _mlp_kernel(x_ref, wu_ref, wd_ref, lng_ref, lnb_ref, o_ref, acc_ref, xb_ref):
    f = pl.program_id(1)

    @pl.when(f == 0)
    def _():
        xb_ref[...] = x_ref[...].astype(BF16)
        acc_ref[...] = jnp.zeros_like(acc_ref)

    h = jnp.maximum(_dot(xb_ref[...], wu_ref[...]), 0.0)
    acc_ref[...] += _dot((h * h).astype(BF16), wd_ref[...])

    @pl.when(f == pl.num_programs(1) - 1)
    def _():
        o_ref[...] = _layer_norm(ALPHA * x_ref[...] + acc_ref[...], lng_ref[...], lnb_ref[...])


def _mlp(x, w_up, w_down, ln_g, ln_b, tm=512, tf=1024):
    m, dm = x.shape
    tm = min(tm, m)
    return pl.pallas_call(
        _mlp_kernel,
        grid=(m // tm, D_FF // tf),
        in_specs=[pl.BlockSpec((tm, dm), lambda i, f: (i, 0)),
                  pl.BlockSpec((dm, tf), lambda i, f: (0, f)),
                  pl.BlockSpec((tf, dm), lambda i, f: (f, 0)),
                  _resident(ln_g.shape), _resident(ln_b.shape)],
        out_specs=pl.BlockSpec((tm, dm), lambda i, f: (i, 0)),
        out_shape=jax.ShapeDtypeStruct((m, dm), F32),
        scratch_shapes=[pltpu.VMEM((tm, dm), F32), pltpu.VMEM((tm, dm), BF16)],
        compiler_params=_cparams(("parallel", "arbitrary")),
        name="mlp_ln3",
    )(x, w_up, w_down, ln_g, ln_b)


def _pad_cols(a, width):
    return jnp.pad(a, [(0, 0)] * (a.ndim - 1) + [(0, width - a.shape[-1])])


def _rw_cols(a):
    w3 = 3 * RW_WIDTH
    return jnp.concatenate([a[..., :w3],
                            _pad_cols(a[..., w3:w3 + DECAY_LORA], LORA_PAD),
                            _pad_cols(a[..., w3 + DECAY_LORA:w3 + DECAY_LORA + ICLR_LORA], LORA_PAD),
                            a[..., w3 + DECAY_LORA + ICLR_LORA:RW_SHIFT_COLS]], axis=-1)


def _prepare(w_in, rw_mu_prev, rw_mu_next, rw_w0, rw_w2, rw_a0, rw_a2, rw_g2, rw_k_k, rw_k_a, rw_r_k,
             rw_gn_g, rw_gn_b, m_conv_w, m_conv_b, m_dt_bias, m_a_log, m_d, m_norm_g, w_br, w_bm, w_o,
             ln1_g, ln1_b, w_q, w_kv, w_co, ln2_g, ln2_b, w_up, w_down, ln3_g, ln3_b):
    l = 0
    w = w_in[l]
    c0 = RW_SHIFT_COLS
    c1 = c0 + M_INNER
    c2 = c1 + M_CONV_CH
    c3 = c2 + M_HEADS
    row = lambda a: a.reshape(1, -1).astype(F32)
    head = jnp.arange(RW_WIDTH) // RW_HEAD
    lane_head = jnp.arange(M_INNER) // M_HEADDIM
    return dict(
        w_tiles=jnp.concatenate([_pad_cols(_rw_cols(w[:, :c0]), COL_XBC), w[:, c1:c2], w[:, c3:], w[:, c0:c1]],
                                axis=1).astype(BF16).reshape(D_MODEL, INPROJ_COLS // INPROJ_TN, INPROJ_TN
                                                             ).transpose(1, 0, 2),
        w_dt=_pad_cols(w[:, c2:c3], LANES).astype(BF16),
        mup=row(_rw_cols(rw_mu_prev[l])), mun=row(_rw_cols(rw_mu_next[l])),
        w0=rw_w0[l].astype(F32), a0=rw_a0[l].astype(F32),
        w2=jnp.pad(rw_w2[l], ((0, LORA_PAD - DECAY_LORA), (0, 0))).astype(BF16),
        a2=jnp.pad(rw_a2[l], ((0, LORA_PAD - ICLR_LORA), (0, 0))).astype(BF16),
        g2=rw_g2[l].astype(BF16),
        k_k=row(rw_k_k[l]), k_a=row(rw_k_a[l]), r_k=row(rw_r_k[l]),
        gn_g=row(rw_gn_g[l]), gn_b=row(rw_gn_b[l]),
        hdown=(head[:, None] == jnp.arange(LANES)[None, :]).astype(BF16),
        hup=(jnp.arange(LANES)[:, None] == head[None, :]).astype(BF16),
        conv_w=m_conv_w[l].astype(F32), conv_b=row(m_conv_b[l]),
        dt_bias=_pad_cols(m_dt_bias[l], LANES).reshape(2, 1, LANES).astype(F32),
        a_log=_pad_cols(m_a_log[l], LANES).reshape(2, 1, LANES).astype(F32),
        hexp=(jnp.arange(LANES)[:, None] == lane_head[None, :]).astype(BF16),
        dskip=row(jnp.repeat(m_d[l], M_HEADDIM)), norm_g=row(m_norm_g[l]),
        w_br=w_br[l].astype(BF16), w_bm=w_bm[l].astype(BF16), w_o=w_o[l].astype(BF16),
        ln1_g=row(ln1_g[l]), ln1_b=row(ln1_b[l]),
        w_q=w_q[l].astype(BF16), w_kv=w_kv[l].astype(BF16), w_co=w_co[l].astype(BF16),
        ln2_g=row(ln2_g[l]), ln2_b=row(ln2_b[l]),
        w_up=w_up[l].astype(BF16), w_down=w_down[l].astype(BF16),
        ln3_g=row(ln3_g[l]), ln3_b=row(ln3_b[l]),
    )


def _encoder_layer(x, mem, p):
    n, t, dm = x.shape
    x2 = x.reshape(n * t, dm)
    as3 = lambda a: a.reshape(n, t, a.shape[-1])
    proj, dt_raw = _inproj(x2, p["w_tiles"], p["w_dt"])
    proj = as3(proj)
    dt_raw = as3(dt_raw)

    r, k, v, kn, g, bonus, ld, a = _rw_prep(proj, p["mup"], p["mun"], p["w2"], p["a2"], p["g2"],
                                            p["w0"], p["a0"], p["k_k"], p["r_k"], p["hdown"], p["hup"])
    yw = _wkv(r, k, v, kn, ld, a, p["k_a"])

    xbc_act = _conv_silu(proj, p["conv_w"], p["conv_b"])
    ym = _ssd(xbc_act, dt_raw, p["dt_bias"], p["a_log"], p["hexp"])

    x1 = _merge(yw, bonus, g, ym, xbc_act, proj, x, p["hdown"], p["hup"], p["gn_g"], p["gn_b"], p["dskip"],
                p["norm_g"], p["w_br"], p["w_bm"], p["w_o"], p["ln1_g"], p["ln1_b"])

    kv = _mm(mem.reshape(n * N_MEM, dm), p["w_kv"]).reshape(n, N_MEM, 2 * dm)
    x2_ = _attention(x1, kv, p["w_q"], p["w_co"], p["ln2_g"], p["ln2_b"])
    out = _mlp(x2_.reshape(n * t, dm), p["w_up"], p["w_down"], p["ln3_g"], p["ln3_b"])
    return out.reshape(n, t, dm)


def kernel(x_prompt, x_sample, mem_prompt, mem_sample, w_in, rw_mu_prev, rw_mu_next, rw_w0, rw_w2, rw_a0, rw_a2, rw_g2, rw_k_k, rw_k_a, rw_r_k, rw_gn_g, rw_gn_b, m_conv_w, m_conv_b, m_dt_bias, m_a_log, m_d, m_norm_g, w_br, w_bm, w_o, ln1_g, ln1_b, w_q, w_kv, w_co, ln2_g, ln2_b, w_up, w_down, ln3_g, ln3_b):
    p = _prepare(w_in, rw_mu_prev, rw_mu_next, rw_w0, rw_w2, rw_a0, rw_a2, rw_g2, rw_k_k, rw_k_a, rw_r_k,
                 rw_gn_g, rw_gn_b, m_conv_w, m_conv_b, m_dt_bias, m_a_log, m_d, m_norm_g, w_br, w_bm, w_o,
                 ln1_g, ln1_b, w_q, w_kv, w_co, ln2_g, ln2_b, w_up, w_down, ln3_g, ln3_b)
    return (_encoder_layer(x_prompt, mem_prompt, p), _encoder_layer(x_sample, mem_sample, p))
```

```python
import functools
import math

import jax
import jax.numpy as jnp
from jax import lax
from jax.experimental import pallas as pl
from jax.experimental.pallas import tpu as pltpu

F32 = jnp.float32
BF16 = jnp.bfloat16

D_MODEL = 2048
RW_HEAD = 64
RW_WIDTH = D_MODEL // 2
RW_HEADS = RW_WIDTH // RW_HEAD
DECAY_LORA = 96
ICLR_LORA = 96
GATE_LORA = 256
LORA_PAD = 128
M_INNER = D_MODEL
M_HEADDIM = 64
M_HEADS = M_INNER // M_HEADDIM
M_STATE = 128
M_GROUPS = 8
CONV_W = 5
N_MEM = 256
X_HEADS = 4
X_HEAD_DIM = D_MODEL // X_HEADS
D_FF = 4 * D_MODEL
DEPTH = 1
ALPHA = (2.0 * DEPTH) ** 0.25
LN_EPS = 1e-5
GN_EPS = 64e-5

RW_SHIFT_COLS = 3 * RW_WIDTH + DECAY_LORA + ICLR_LORA + GATE_LORA
RW_PAD_COLS = 3 * RW_WIDTH + 2 * LORA_PAD + GATE_LORA
M_CONV_CH = M_INNER + 2 * M_GROUPS * M_STATE

WKV_CHUNK = 64
SSD_CHUNK = 128
HALO = 16
LANES = 128
VMEM_LIMIT = 56 * 1024 * 1024


def _cparams(sem):
    return pltpu.CompilerParams(dimension_semantics=sem, vmem_limit_bytes=VMEM_LIMIT)


def _resident(shape):
    nd = len(shape)
    return pl.BlockSpec(shape, lambda *_: (0,) * nd, pipeline_mode=pl.Buffered(1))


def _split3(a):
    a1 = a.astype(BF16)
    r1 = a - a1.astype(F32)
    a2 = r1.astype(BF16)
    a3 = (r1 - a2.astype(F32)).astype(BF16)
    return a1, a2, a3


def _dot(a, b):
    return jnp.dot(a, b, preferred_element_type=F32)


def _dot_nt(a, b):
    return lax.dot_general(a, b, (((1,), (1,)), ((), ())), preferred_element_type=F32)


def _sel_left(m01, a):
    a1, a2, a3 = _split3(a)
    return _dot(m01, a1) + _dot(m01, a2) + _dot(m01, a3)


def _sel_right(a, m01, pieces=2):
    a1, a2, a3 = _split3(a)
    out = _dot(a1, m01) + _dot(a2, m01)
    if pieces == 3:
        out = out + _dot(a3, m01)
    return out


def _head_sum(a, down, up):
    return _sel_right(_sel_right(a, down), up)


def _softplus(x):
    return jnp.maximum(x, 0.0) + jnp.log1p(jnp.exp(-jnp.abs(x)))


def _sigmoid(x):
    return 1.0 / (1.0 + jnp.exp(-x))


def _layer_norm(x, g, b):
    mu = jnp.mean(x, axis=-1, keepdims=True)
    xc = x - mu
    var = jnp.mean(xc * xc, axis=-1, keepdims=True)
    return xc * lax.rsqrt(var + LN_EPS) * g + b


def _mm_kernel(x_ref, w_ref, o_ref, xb_ref):
    @pl.when(pl.program_id(1) == 0)
    def _():
        xb_ref[...] = x_ref[...].astype(BF16)

    o_ref[...] = _dot(xb_ref[...], w_ref[...]).astype(o_ref.dtype)


def _mm(x, w, *, out_dtype=BF16, tm=1024, tn=512):
    m, k = x.shape
    n = w.shape[1]
    tm = min(tm, m)
    tn = min(tn, n)
    return pl.pallas_call(
        _mm_kernel,
        grid=(m // tm, n // tn),
        in_specs=[pl.BlockSpec((tm, k), lambda i, j: (i, 0)),
                  pl.BlockSpec((k, tn), lambda i, j: (0, j))],
        out_specs=pl.BlockSpec((tm, tn), lambda i, j: (i, j)),
        out_shape=jax.ShapeDtypeStruct((m, n), out_dtype),
        scratch_shapes=[pltpu.VMEM((tm, k), BF16)],
        compiler_params=_cparams(("parallel", "arbitrary")),
        name="matmul",
    )(x, w)


INPROJ_TN = 1024
INPROJ_GATE_CHUNK = 256
COL_RW = 0
COL_XBC = 4 * INPROJ_TN
COL_GATES = COL_XBC + M_CONV_CH
COL_Z = COL_GATES + 2 * D_MODEL
INPROJ_COLS = COL_Z + M_INNER


def _inproj_kernel(x_ref, w_ref, wdt_ref, o_ref, dt_ref, xb_ref):
    j = pl.program_id(1)

    @pl.when(j == 0)
    def _():
        xb = x_ref[...].astype(BF16)
        xb_ref[...] = xb
        dt_ref[...] = _dot(xb, wdt_ref[...])

    is_gate = jnp.logical_and(j >= COL_GATES // INPROJ_TN, j < COL_Z // INPROJ_TN)

    @pl.when(is_gate)
    def _():
        for c0 in range(0, INPROJ_TN, INPROJ_GATE_CHUNK):
            cl = slice(c0, c0 + INPROJ_GATE_CHUNK)
            o_ref[:, cl] = _sigmoid(_dot(xb_ref[...], w_ref[0, :, cl])).astype(o_ref.dtype)

    @pl.when(jnp.logical_not(is_gate))
    def _():
        o_ref[...] = _dot(xb_ref[...], w_ref[0]).astype(o_ref.dtype)


def _inproj(x, w_tiles, w_dt, tm=1024):
    m, k = x.shape
    tm = min(tm, m)
    tn = INPROJ_TN
    return pl.pallas_call(
        _inproj_kernel,
        grid=(m // tm, INPROJ_COLS // tn),
        in_specs=[pl.BlockSpec((tm, k), lambda i, j: (i, 0)),
                  pl.BlockSpec((1, k, tn), lambda i, j: (j, 0, 0)),
                  _resident(w_dt.shape)],
        out_specs=[pl.BlockSpec((tm, tn), lambda i, j: (i, j)),
                   pl.BlockSpec((tm, LANES), lambda i, j: (i, 0))],
        out_shape=[jax.ShapeDtypeStruct((m, INPROJ_COLS), BF16), jax.ShapeDtypeStruct((m, LANES), F32)],
        scratch_shapes=[pltpu.VMEM((tm, k), BF16)],
        compiler_params=_cparams(("parallel", "arbitrary")),
        name="in_projection",
    )(x, w_tiles, w_dt)


def _halo_specs(t, tt, width):
    per = tt // HALO
    last = t // HALO - 1

    def prev_map(b, i):
        return (b, jnp.maximum(i * per - 1, 0), 0)

    def next_map(b, i):
        return (b, jnp.minimum((i + 1) * per, last), 0)

    return (pl.BlockSpec((1, HALO, width), prev_map), pl.BlockSpec((1, HALO, width), next_map))


SHIFT_ROWS = 128
SHIFT_LANES = 256


def _shift_matrix(rb, off):
    row = lax.broadcasted_iota(jnp.int32, (rb, rb + 2 * HALO), 0)
    col = lax.broadcasted_iota(jnp.int32, (rb, rb + 2 * HALO), 1)
    return jnp.where(col - row == HALO + off, 1.0, 0.0).astype(BF16)


def _extended_block(x_ref, prev_rows, next_rows, r0, cl):
    tt = x_ref.shape[1]
    rb = SHIFT_ROWS
    above = prev_rows[:, cl] if r0 == 0 else x_ref[0, r0 - HALO:r0, cl]
    below = next_rows[:, cl] if r0 + rb == tt else x_ref[0, r0 + rb:r0 + rb + HALO, cl]
    xb = x_ref[0, r0:r0 + rb, cl]
    return xb, jnp.concatenate([above, xb, below], axis=0)


def _rw_prep_kernel(p_ref, pp_ref, pn_ref, mup_ref, mun_ref, w2_ref, a2_ref, g2_ref, w0_ref, a0_ref,
                    kk_ref, rk_ref, hdown_ref, hup_ref,
                    r_out, k_out, v_out, kn_out, g_out, bonus_out, ld_out, a_out, ps_ref):
    i = pl.program_id(1)
    nt = pl.num_programs(1)
    tt, c = p_ref.shape[1], p_ref.shape[2]
    prev_rows = jnp.where(i > 0, pp_ref[0], jnp.zeros_like(pp_ref[0]))
    next_rows = jnp.where(i < nt - 1, pn_ref[0], jnp.zeros_like(pn_ref[0]))
    s_prev = _shift_matrix(SHIFT_ROWS, -1)
    s_next = _shift_matrix(SHIFT_ROWS, 1)
    for r0 in range(0, tt, SHIFT_ROWS):
        for c0 in range(0, c, SHIFT_LANES):
            cl = slice(c0, c0 + SHIFT_LANES)
            pb, pe = _extended_block(p_ref, prev_rows, next_rows, r0, cl)
            p = pb.astype(F32)
            ps_ref[r0:r0 + SHIFT_ROWS, cl] = (p + mup_ref[:, cl] * (_dot(s_prev, pe) - p)
                                               + mun_ref[:, cl] * (_dot(s_next, pe) - p))
    w = RW_WIDTH
    r = ps_ref[:, 0:w]
    k = ps_ref[:, w:2 * w]
    v = ps_ref[:, 2 * w:3 * w]
    dw = ps_ref[:, 3 * w:3 * w + LORA_PAD]
    da = ps_ref[:, 3 * w + LORA_PAD:3 * w + 2 * LORA_PAD]
    dg = ps_ref[:, 3 * w + 2 * LORA_PAD:]
    hw = _dot(jnp.tanh(dw).astype(BF16), w2_ref[...])
    ha = _dot(da.astype(BF16), a2_ref[...])
    g = _dot(_sigmoid(dg).astype(BF16), g2_ref[...])
    for d in range(2):
        ld_out[d, 0] = -math.exp(-0.5) * _sigmoid(w0_ref[d:d + 1, :] + hw)
        a_out[d, 0] = _sigmoid(a0_ref[d:d + 1, :] + ha)
    hdown = hdown_ref[...]
    hup = hup_ref[...]
    kkr = k * kk_ref[...]
    ss = _head_sum(kkr * kkr, hdown, hup)
    kn = kkr * lax.rsqrt(jnp.maximum(ss, 1e-24))
    rk = _head_sum(r * k * rk_ref[...], hdown, hup)
    r_out[0] = r.astype(r_out.dtype)
    k_out[0] = k.astype(k_out.dtype)
    v_out[0] = v.astype(v_out.dtype)
    kn_out[0] = kn.astype(kn_out.dtype)
    g_out[0] = g.astype(g_out.dtype)
    bonus_out[0] = (rk * v).astype(bonus_out.dtype)


def _rw_prep(p, mup, mun, w2, a2, g2, w0, a0, k_k, r_k, hdown, hup, tt=256):
    n, t, _ = p.shape
    c = RW_PAD_COLS
    tt = min(tt, t)
    w = RW_WIDTH
    tile = lambda width: pl.BlockSpec((1, tt, width), lambda b, i: (b, i, 0))
    dir_tile = pl.BlockSpec((2, 1, tt, w), lambda b, i: (0, b, i, 0))
    hp, hn = _halo_specs(t, tt, c)
    bf = jax.ShapeDtypeStruct((n, t, w), BF16)
    dd = jax.ShapeDtypeStruct((2, n, t, w), F32)
    return pl.pallas_call(
        _rw_prep_kernel,
        grid=(n, t // tt),
        in_specs=[tile(c), hp, hn, _resident(mup.shape), _resident(mun.shape), _resident(w2.shape),
                  _resident(a2.shape), _resident(g2.shape), _resident(w0.shape), _resident(a0.shape),
                  _resident(k_k.shape), _resident(r_k.shape), _resident(hdown.shape), _resident(hup.shape)],
        out_specs=[tile(w)] * 6 + [dir_tile, dir_tile],
        out_shape=[bf] * 6 + [dd, dd],
        scratch_shapes=[pltpu.VMEM((tt, c), F32)],
        compiler_params=_cparams(("parallel", "parallel")),
        name="rwkv_prep",
    )(p, p, p, mup, mun, w2, a2, g2, w0, a0, k_k, r_k, hdown, hup)


def _wkv_kernel(r_ref, k_ref, v_ref, kn_ref, ld_ref, a_ref, ka_ref, y_ref, h_ref, *, nchunk):
    d = pl.program_id(1)
    c = pl.program_id(2)

    @pl.when(c == 0)
    def _():
        h_ref[...] = jnp.zeros_like(h_ref)

    cs = WKV_CHUNK
    npair = RW_HEADS // 2
    sgn = 1 - 2 * d
    row = lax.broadcasted_iota(jnp.int32, (cs, 2 * cs), 0)
    col = lax.broadcasted_iota(jnp.int32, (cs, 2 * cs), 1) & (cs - 1)
    lag2 = (row - col) * sgn
    strict2 = lag2 > 0
    incl2 = lag2 >= 0
    m_incl = jnp.where(incl2[:, :cs], 1.0, 0.0).astype(BF16)
    eye = jnp.where(lax.broadcasted_iota(jnp.int32, (cs, cs), 0)
                    == lax.broadcasted_iota(jnp.int32, (cs, cs), 1), 1.0, 0.0)
    lane = lax.broadcasted_iota(jnp.int32, (1, LANES), 1)
    left = lane < RW_HEAD
    right = lane >= RW_HEAD
    left2 = (lax.broadcasted_iota(jnp.int32, (cs, 2 * LANES), 1) & (LANES - 1)) < RW_HEAD
    blockdiag = (lax.broadcasted_iota(jnp.int32, (LANES, LANES), 0) // RW_HEAD
                 == lax.broadcasted_iota(jnp.int32, (LANES, LANES), 1) // RW_HEAD)
    eye_l = jnp.where(lax.broadcasted_iota(jnp.int32, (LANES, LANES), 0)
                      == lax.broadcasted_iota(jnp.int32, (LANES, LANES), 1), 1.0, 0.0).astype(BF16)
    zeros_cv = jnp.zeros((cs, LANES), BF16)
    zeros_cf = jnp.zeros((cs, LANES), F32)
    ka = ka_ref[...]

    offs, prep = [], []
    for j in range(nchunk):
        off = pl.multiple_of((j + d * (nchunk - 1 - 2 * j)) * cs, cs)
        offs.append(off)
        ld = ld_ref[0, 0, pl.ds(off, cs), :]
        cum = _sel_left(m_incl, ld)
        tot = jnp.sum(ld, axis=0, keepdims=True)
        r = r_ref[0, pl.ds(off, cs), :].astype(F32)
        k = k_ref[0, pl.ds(off, cs), :].astype(F32)
        kn = kn_ref[0, pl.ds(off, cs), :].astype(F32)
        a = a_ref[0, 0, pl.ds(off, cs), :]
        kdir = k * (1.0 + (a - 1.0) * ka)
        b = kn * a
        e_ncum = jnp.exp(-cum)
        e_rem = jnp.exp(tot - cum)
        prep.append(dict(
            rt=(r * jnp.exp(cum)).astype(BF16), at=(-kn * jnp.exp(cum - ld)).astype(BF16),
            kt=(kdir * e_ncum).astype(BF16), bt=(b * e_ncum).astype(BF16),
            kh=(kdir * e_rem).astype(BF16), bh=(b * e_rem).astype(BF16),
            v=v_ref[0, pl.ds(off, cs), :], wc=jnp.exp(tot)))

    units = [(j, p) for j in range(nchunk) for p in range(npair)]
    heads = [(j, p, hh) for (j, p) in units for hh in range(2)]
    sl = lambda p: slice(p * LANES, (p + 1) * LANES)

    g_up, g_low, a_ab = {}, {}, {}
    for (j, p) in units:
        q = prep[j]
        ar = jnp.concatenate([q["at"][:, sl(p)], q["rt"][:, sl(p)]], axis=0)
        bk = jnp.concatenate([q["bt"][:, sl(p)], q["kt"][:, sl(p)]], axis=0)
        for hh in range(2):
            arm = jnp.where(left if hh == 0 else right, ar, jnp.zeros_like(ar))
            gm = _dot_nt(arm, bk)
            gu = jnp.where(strict2, gm[:cs], 0.0)
            a_ab[j, p, hh] = gu[:, :cs]
            g_up[j, p, hh] = gu.astype(BF16)
            g_low[j, p, hh] = jnp.where(incl2, gm[cs:], 0.0).astype(BF16)

    tinv = {h: a_ab[h] + eye for h in heads}
    pw = dict(a_ab)
    for _ in range(int(math.log2(cs)) - 1):
        for h in heads:
            pb = pw[h].astype(BF16)
            pw[h] = _dot(pb, pb)
        for h in heads:
            tinv[h] = tinv[h] + _dot(tinv[h].astype(BF16), pw[h].astype(BF16))

    w1 = {}
    for (j, p) in units:
        zv = jnp.concatenate([zeros_cv, prep[j]["v"][:, sl(p)]], axis=0)
        for hh in range(2):
            w1[j, p, hh] = _dot(g_up[j, p, hh], zv)
    tx = {}
    for (j, p, hh) in heads:
        rhs = jnp.concatenate([w1[j, p, hh].astype(BF16), prep[j]["at"][:, sl(p)]], axis=1)
        tx[j, p, hh] = _dot(tinv[j, p, hh].astype(BF16), rhs)
    lmat = {}
    for (j, p) in units:
        va = jnp.where(left2, tx[j, p, 0], tx[j, p, 1])
        vrow = jnp.concatenate([prep[j]["v"][:, sl(p)].astype(F32), zeros_cf], axis=1)
        lmat[j, p] = jnp.concatenate([va, vrow], axis=0).astype(BF16)

    z = {h: _dot(g_low[h], lmat[h[0], h[1]]) for h in heads}
    lv_t = {u: _dot_nt(eye_l, lmat[u][:, :LANES]).astype(BF16) for u in units}
    bh_t = {(j, p): _dot_nt(eye_l, prep[j]["bh"][:, sl(p)]).astype(BF16) for (j, p) in units}
    y0, m1, m2, n2t = {}, {}, {}, {}
    for (j, p) in units:
        q = prep[j]
        ym = jnp.where(left2, z[j, p, 0], z[j, p, 1])
        y0[j, p] = ym[:, :LANES]
        m1[j, p] = (q["rt"][:, sl(p)].astype(F32) + ym[:, LANES:]).astype(BF16)
        bkh = jnp.concatenate([q["bh"][:, sl(p)], q["kh"][:, sl(p)]], axis=0)
        n2t[j, p] = jnp.where(blockdiag, _dot(lv_t[j, p], bkh), 0.0)
    for (j, p) in units:
        m2[j, p] = jnp.where(blockdiag, _dot(bh_t[j, p], lmat[j, p][:cs, LANES:]), 0.0).astype(BF16)

    hts = [h_ref[p] for p in range(npair)]
    for j in range(nchunk):
        for p in range(npair):
            htb = hts[p].astype(BF16)
            y = y0[j, p] + _dot_nt(m1[j, p], htb)
            y_ref[0, 0, pl.ds(offs[j], cs), sl(p)] = y.astype(y_ref.dtype)
            hts[p] = hts[p] * prep[j]["wc"][:, sl(p)] + _dot_nt(htb, m2[j, p]) + n2t[j, p]
    for p in range(npair):
        h_ref[p] = hts[p]


def _wkv(r, k, v, kn, ld, a, k_a, nchunk=4):
    n, t, w = r.shape
    tb = WKV_CHUNK * nchunk
    nb = t // tb

    def tmap(b, d, c):
        return (b, c + d * (nb - 1 - 2 * c), 0)

    def dmap(b, d, c):
        return (d, b, c + d * (nb - 1 - 2 * c), 0)

    tile = pl.BlockSpec((1, tb, w), tmap)
    dtile = pl.BlockSpec((1, 1, tb, w), dmap)
    return pl.pallas_call(
        functools.partial(_wkv_kernel, nchunk=nchunk),
        grid=(n, 2, nb),
        in_specs=[tile, tile, tile, tile, dtile, dtile, _resident(k_a.shape)],
        out_specs=dtile,
        out_shape=jax.ShapeDtypeStruct((2, n, t, w), BF16),
        scratch_shapes=[pltpu.VMEM((RW_HEADS // 2, LANES, LANES), F32)],
        compiler_params=_cparams(("parallel", "arbitrary", "arbitrary")),
        name="wkv7_scan",
    )(r, k, v, kn, ld, a, k_a)


def _conv_kernel(x_ref, xp_ref, xn_ref, w_ref, b_ref, o_ref):
    i = pl.program_id(1)
    nt = pl.num_programs(1)
    tt, tc = x_ref.shape[1], x_ref.shape[2]
    half = CONV_W // 2
    rb = SHIFT_ROWS
    shifts = {j: _shift_matrix(rb, j - half) for j in range(CONV_W) if j != half}
    prev_rows = jnp.where(i > 0, xp_ref[0], jnp.zeros_like(xp_ref[0]))
    next_rows = jnp.where(i < nt - 1, xn_ref[0], jnp.zeros_like(xn_ref[0]))
    for r0 in range(0, tt, rb):
        for c0 in range(0, tc, SHIFT_LANES):
            cl = slice(c0, c0 + SHIFT_LANES)
            xb, xe = _extended_block(x_ref, prev_rows, next_rows, r0, cl)
            acc = xb.astype(F32) * w_ref[half:half + 1, cl] + b_ref[:, cl]
            for j, sh in shifts.items():
                acc = acc + _dot(sh, xe) * w_ref[j:j + 1, cl]
            o_ref[0, r0:r0 + rb, cl] = (acc * _sigmoid(acc)).astype(o_ref.dtype)


def _conv_silu(proj, w, b, tt=256, tc=1024):
    n, t, _ = proj.shape
    c = M_CONV_CH
    j0 = COL_XBC // tc
    tt = min(tt, t)
    per = tt // HALO
    nrow = t // HALO
    return pl.pallas_call(
        _conv_kernel,
        grid=(n, t // tt, c // tc),
        in_specs=[pl.BlockSpec((1, tt, tc), lambda b_, i, j: (b_, i, j0 + j)),
                  pl.BlockSpec((1, HALO, tc), lambda b_, i, j: (b_, jnp.maximum(i * per - 1, 0), j0 + j)),
                  pl.BlockSpec((1, HALO, tc), lambda b_, i, j: (b_, jnp.minimum((i + 1) * per, nrow - 1), j0 + j)),
                  pl.BlockSpec((CONV_W, tc), lambda b_, i, j: (0, j)),
                  pl.BlockSpec((1, tc), lambda b_, i, j: (0, j))],
        out_specs=pl.BlockSpec((1, tt, tc), lambda b_, i, j: (b_, i, j)),
        out_shape=jax.ShapeDtypeStruct((n, t, c), BF16),
        compiler_params=_cparams(("parallel", "parallel", "parallel")),
        name="mamba_conv",
    )(proj, proj, proj, w, b)


def _ssd_kernel(xbc_ref, dt_ref, dtb_ref, alog_ref, hexp_ref, y_ref, h_ref):
    d = pl.program_id(1)
    c = pl.program_id(2)

    @pl.when(c == 0)
    def _():
        h_ref[...] = jnp.zeros_like(h_ref)

    ln = SSD_CHUNK
    row = lax.broadcasted_iota(jnp.int32, (ln, ln), 0)
    col = lax.broadcasted_iota(jnp.int32, (ln, ln), 1)
    incl = (row - col) * (1 - 2 * d) >= 0
    m_incl = jnp.where(incl, 1.0, 0.0).astype(BF16)

    dt = _softplus(dt_ref[0] + dtb_ref[0])
    da = dt * (-jnp.exp(alog_ref[0]))
    cum = _sel_left(m_incl, da)
    tot = jnp.sum(da, axis=0, keepdims=True)
    cum_t = cum.T
    hexp = hexp_ref[...]
    dt_x = _dot(dt.astype(BF16), hexp)
    dst_x = _dot((jnp.exp(tot - cum) * dt).astype(BF16), hexp)
    ecum_x = _dot(jnp.exp(cum).astype(BF16), hexp)
    cdec = jnp.exp(jnp.broadcast_to(tot, (8, LANES)))
    cdec_x = _sel_right(cdec, hexp, pieces=3)[0:1]

    x = xbc_ref[0, :, 0:M_INNER].astype(F32)
    xdt = (x * dt_x).astype(BF16)
    xdst = (x * dst_x).astype(BF16)

    lane = lax.broadcasted_iota(jnp.int32, (1, LANES), 1)
    left = lane < M_HEADDIM
    eye_l = jnp.where(row == col, 1.0, 0.0).astype(BF16)
    gw = M_INNER // M_GROUPS
    hpg = M_HEADS // M_GROUPS
    b_off = M_INNER
    c_off = M_INNER + M_GROUPS * M_STATE
    groups = range(M_GROUPS)
    gsl = lambda g: slice(g * gw, (g + 1) * gw)
    bg = {g: xbc_ref[0, :, b_off + g * M_STATE:b_off + (g + 1) * M_STATE] for g in groups}
    cg = {g: xbc_ref[0, :, c_off + g * M_STATE:c_off + (g + 1) * M_STATE] for g in groups}
    cb = {g: _dot_nt(cg[g], bg[g]) for g in groups}
    hg = {g: h_ref[g] for g in groups}
    y_in = {g: _dot(cg[g], hg[g].astype(BF16)) * ecum_x[:, gsl(g)] for g in groups}
    bg_t = {g: _dot_nt(eye_l, bg[g]).astype(BF16) for g in groups}
    for g in groups:
        h_ref[g] = hg[g] * cdec_x[:, gsl(g)] + _dot(bg_t[g], xdst[:, gsl(g)])
    for g in groups:
        for pr in range(hpg // 2):
            lo = g * gw + pr * LANES
            ys = []
            for hh in range(2):
                e = g * hpg + 2 * pr + hh
                seg = cum[:, e:e + 1] - cum_t[e:e + 1, :]
                decay = jnp.exp(jnp.where(incl, seg, -1e30))
                ys.append(_dot((cb[g] * decay).astype(BF16), xdt[:, lo:lo + LANES]))
            y = jnp.where(left, ys[0], ys[1]) + y_in[g][:, pr * LANES:(pr + 1) * LANES]
            y_ref[0, 0, :, lo:lo + LANES] = y.astype(y_ref.dtype)


def _ssd(xbc_act, dt_raw, dt_bias, a_log, hexp):
    n, t, c = xbc_act.shape
    ln = SSD_CHUNK
    nc = t // ln

    def tmap(b, d, ci):
        return (b, ci + d * (nc - 1 - 2 * ci), 0)

    def dmap(b, d, ci):
        return (d, b, ci + d * (nc - 1 - 2 * ci), 0)

    par = pl.BlockSpec((1, 1, LANES), lambda b, d, ci: (d, 0, 0))
    return pl.pallas_call(
        _ssd_kernel,
        grid=(n, 2, nc),
        in_specs=[pl.BlockSpec((1, ln, c), tmap), pl.BlockSpec((1, ln, LANES), tmap), par, par,
                  _resident(hexp.shape)],
        out_specs=pl.BlockSpec((1, 1, ln, M_INNER), dmap),
        out_shape=jax.ShapeDtypeStruct((2, n, t, M_INNER), BF16),
        scratch_shapes=[pltpu.VMEM((M_GROUPS, M_STATE, M_INNER // M_GROUPS), F32)],
        compiler_params=_cparams(("parallel", "arbitrary", "arbitrary")),
        name="ssd_scan",
    )(xbc_act, dt_raw, dt_bias, a_log, hexp)


def _merge_kernel(yw_ref, bonus_ref, g_ref, ym_ref, xs_ref, z_ref, gates_ref, x_ref,
                  hdown_ref, hup_ref, gng_ref, gnb_ref, dskip_ref, ng_ref, wbr_ref, wbm_ref, wo_ref,
                  lng_ref, lnb_ref, o_ref):
    y = yw_ref[0, 0].astype(F32) + yw_ref[1, 0].astype(F32)
    hdown = hdown_ref[...]
    hup = hup_ref[...]
    mu = _head_sum(y, hdown, hup) * (1.0 / RW_HEAD)
    yc = y - mu
    var = _head_sum(yc * yc, hdown, hup) * (1.0 / RW_HEAD)
    yn = yc * lax.rsqrt(var + GN_EPS) * gng_ref[...] + gnb_ref[...]
    rw = (yn + bonus_ref[0].astype(F32)) * g_ref[0].astype(F32)
    z = z_ref[0].astype(F32)
    ym = ym_ref[0, 0].astype(F32) + ym_ref[1, 0].astype(F32) + dskip_ref[...] * xs_ref[0].astype(F32)
    ym = ym * (z * _sigmoid(z))
    gw = M_INNER // M_GROUPS
    parts = []
    for gi in range(M_GROUPS):
        yg = ym[:, gi * gw:(gi + 1) * gw]
        ms = jnp.mean(yg * yg, axis=-1, keepdims=True)
        parts.append(yg * lax.rsqrt(ms + LN_EPS))
    mo = jnp.concatenate(parts, axis=1) * ng_ref[...]
    u_rw = _dot(rw.astype(BF16), wbr_ref[...])
    u_m = _dot(mo.astype(BF16), wbm_ref[...])
    gates = gates_ref[0]
    mixed = gates[:, :D_MODEL].astype(F32) * u_rw + gates[:, D_MODEL:].astype(F32) * u_m
    mix = _dot(mixed.astype(BF16), wo_ref[...])
    o_ref[0] = _layer_norm(ALPHA * x_ref[0] + mix, lng_ref[...], lnb_ref[...])


def _merge(yw, bonus, g, ym, xbc_act, proj, x, hdown, hup, gn_g, gn_b, dskip, norm_g, w_br, w_bm, w_o,
           ln_g, ln_b, tm=256):
    n, t, _ = x.shape
    tm = min(tm, t)
    tile = lambda width: pl.BlockSpec((1, tm, width), lambda b, i: (b, i, 0))
    dtile = lambda width: pl.BlockSpec((2, 1, tm, width), lambda b, i: (0, b, i, 0))
    res = [hdown, hup, gn_g, gn_b, dskip, norm_g, w_br, w_bm, w_o, ln_g, ln_b]
    return pl.pallas_call(
        _merge_kernel,
        grid=(n, t // tm),
        in_specs=[dtile(RW_WIDTH), tile(RW_WIDTH), tile(RW_WIDTH), dtile(M_INNER), tile(M_INNER),
                  pl.BlockSpec((1, tm, M_INNER), lambda b, i: (b, i, COL_Z // M_INNER)),
                  pl.BlockSpec((1, tm, 2 * D_MODEL), lambda b, i: (b, i, COL_GATES // (2 * D_MODEL))),
                  tile(D_MODEL)] + [_resident(a.shape) for a in res],
        out_specs=tile(D_MODEL),
        out_shape=jax.ShapeDtypeStruct(x.shape, F32),
        compiler_params=_cparams(("parallel", "parallel")),
        name="merge_ln1",
    )(yw, bonus, g, ym, xbc_act, proj, proj, x, *res)


def _attn_kernel(x_ref, kv_ref, wq_ref, wco_ref, lng_ref, lnb_ref, o_ref):
    x = x_ref[0]
    q = _dot(x.astype(BF16), wq_ref[...]).astype(BF16)
    scale = 1.0 / math.sqrt(X_HEAD_DIM)
    hs = range(X_HEADS)
    sl = lambda h: slice(h * X_HEAD_DIM, (h + 1) * X_HEAD_DIM)
    s = [_dot_nt(q[:, sl(h)], kv_ref[0, :, sl(h)]) * scale for h in hs]
    pr = []
    for h in hs:
        e = jnp.exp(s[h] - jnp.max(s[h], axis=-1, keepdims=True))
        pr.append((e / jnp.sum(e, axis=-1, keepdims=True)).astype(BF16))
    o = [_dot(pr[h], kv_ref[0, :, D_MODEL + h * X_HEAD_DIM:D_MODEL + (h + 1) * X_HEAD_DIM]).astype(BF16) for h in hs]
    out = _dot(jnp.concatenate(o, axis=1), wco_ref[...])
    o_ref[0] = _layer_norm(ALPHA * x + out, lng_ref[...], lnb_ref[...])


def _attention(x, kv, w_q, w_co, ln_g, ln_b, tm=512):
    n, t, _ = x.shape
    tm = min(tm, t)
    tile = pl.BlockSpec((1, tm, D_MODEL), lambda b, i: (b, i, 0))
    return pl.pallas_call(
        _attn_kernel,
        grid=(n, t // tm),
        in_specs=[tile, pl.BlockSpec((1, N_MEM, 2 * D_MODEL), lambda b, i: (b, 0, 0)),
                  _resident(w_q.shape), _resident(w_co.shape), _resident(ln_g.shape), _resident(ln_b.shape)],
        out_specs=tile,
        out_shape=jax.ShapeDtypeStruct(x.shape, F32),
        compiler_params=_cparams(("parallel", "parallel")),
        name="mem_attention_ln2",
    )(x, kv, w_q, w_co, ln_g, ln_b)


def _mlp_kernel(x_ref, wu_ref, wd_ref, lng_ref, lnb_ref, o_ref, acc_ref, xb_ref):
    f = pl.program_id(1)

    @pl.when(f == 0)
    def _():
        xb_ref[...] = x_ref[...].astype(BF16)
        acc_ref[...] = jnp.zeros_like(acc_ref)

    h = jnp.maximum(_dot(xb_ref[...], wu_ref[...]), 0.0)
    acc_ref[...] += _dot((h * h).astype(BF16), wd_ref[...])

    @pl.when(f == pl.num_programs(1) - 1)
    def _():
        o_ref[...] = _layer_norm(ALPHA * x_ref[...] + acc_ref[...], lng_ref[...], lnb_ref[...])


def _mlp(x, w_up, w_down, ln_g, ln_b, tm=512, tf=1024):
    m, dm = x.shape
    tm = min(tm, m)
    return pl.pallas_call(
        _mlp_kernel,
        grid=(m // tm, D_FF // tf),
        in_specs=[pl.BlockSpec((tm, dm), lambda i, f: (i, 0)),
                  pl.BlockSpec((dm, tf), lambda i, f: (0, f)),
                  pl.BlockSpec((tf, dm), lambda i, f: (f, 0)),
                  _resident(ln_g.shape), _resident(ln_b.shape)],
        out_specs=pl.BlockSpec((tm, dm), lambda i, f: (i, 0)),
        out_shape=jax.ShapeDtypeStruct((m, dm), F32),
        scratch_shapes=[pltpu.VMEM((tm, dm), F32), pltpu.VMEM((tm, dm), BF16)],
        compiler_params=_cparams(("parallel", "arbitrary")),
        name="mlp_ln3",
    )(x, w_up, w_down, ln_g, ln_b)


def _pad_cols(a, width):
    return jnp.pad(a, [(0, 0)] * (a.ndim - 1) + [(0, width - a.shape[-1])])


def _rw_cols(a):
    w3 = 3 * RW_WIDTH
    return jnp.concatenate([a[..., :w3],
                            _pad_cols(a[..., w3:w3 + DECAY_LORA], LORA_PAD),
                            _pad_cols(a[..., w3 + DECAY_LORA:w3 + DECAY_LORA + ICLR_LORA], LORA_PAD),
                            a[..., w3 + DECAY_LORA + ICLR_LORA:RW_SHIFT_COLS]], axis=-1)


def _prepare(w_in, rw_mu_prev, rw_mu_next, rw_w0, rw_w2, rw_a0, rw_a2, rw_g2, rw_k_k, rw_k_a, rw_r_k,
             rw_gn_g, rw_gn_b, m_conv_w, m_conv_b, m_dt_bias, m_a_log, m_d, m_norm_g, w_br, w_bm, w_o,
             ln1_g, ln1_b, w_q, w_kv, w_co, ln2_g, ln2_b, w_up, w_down, ln3_g, ln3_b):
    l = 0
    w = w_in[l]
    c0 = RW_SHIFT_COLS
    c1 = c0 + M_INNER
    c2 = c1 + M_CONV_CH
    c3 = c2 + M_HEADS
    row = lambda a: a.reshape(1, -1).astype(F32)
    head = jnp.arange(RW_WIDTH) // RW_HEAD
    lane_head = jnp.arange(M_INNER) // M_HEADDIM
    return dict(
        w_tiles=jnp.concatenate([_pad_cols(_rw_cols(w[:, :c0]), COL_XBC), w[:, c1:c2], w[:, c3:], w[:, c0:c1]],
                                axis=1).astype(BF16).reshape(D_MODEL, INPROJ_COLS // INPROJ_TN, INPROJ_TN
                                                             ).transpose(1, 0, 2),
        w_dt=_pad_cols(w[:, c2:c3], LANES).astype(BF16),
        mup=row(_rw_cols(rw_mu_prev[l])), mun=row(_rw_cols(rw_mu_next[l])),
        w0=rw_w0[l].astype(F32), a0=rw_a0[l].astype(F32),
        w2=jnp.pad(rw_w2[l], ((0, LORA_PAD - DECAY_LORA), (0, 0))).astype(BF16),
        a2=jnp.pad(rw_a2[l], ((0, LORA_PAD - ICLR_LORA), (0, 0))).astype(BF16),
        g2=rw_g2[l].astype(BF16),
        k_k=row(rw_k_k[l]), k_a=row(rw_k_a[l]), r_k=row(rw_r_k[l]),
        gn_g=row(rw_gn_g[l]), gn_b=row(rw_gn_b[l]),
        hdown=(head[:, None] == jnp.arange(LANES)[None, :]).astype(BF16),
        hup=(jnp.arange(LANES)[:, None] == head[None, :]).astype(BF16),
        conv_w=m_conv_w[l].astype(F32), conv_b=row(m_conv_b[l]),
        dt_bias=_pad_cols(m_dt_bias[l], LANES).reshape(2, 1, LANES).astype(F32),
        a_log=_pad_cols(m_a_log[l], LANES).reshape(2, 1, LANES).astype(F32),
        hexp=(jnp.arange(LANES)[:, None] == lane_head[None, :]).astype(BF16),
        dskip=row(jnp.repeat(m_d[l], M_HEADDIM)), norm_g=row(m_norm_g[l]),
        w_br=w_br[l].astype(BF16), w_bm=w_bm[l].astype(BF16), w_o=w_o[l].astype(BF16),
        ln1_g=row(ln1_g[l]), ln1_b=row(ln1_b[l]),
        w_q=w_q[l].astype(BF16), w_kv=w_kv[l].astype(BF16), w_co=w_co[l].astype(BF16),
        ln2_g=row(ln2_g[l]), ln2_b=row(ln2_b[l]),
        w_up=w_up[l].astype(BF16), w_down=w_down[l].astype(BF16),
        ln3_g=row(ln3_g[l]), ln3_b=row(ln3_b[l]),
    )


def _encoder_layer(x, mem, p):
    n, t, dm = x.shape
    x2 = x.reshape(n * t, dm)
    as3 = lambda a: a.reshape(n, t, a.shape[-1])
    proj, dt_raw = _inproj(x2, p["w_tiles"], p["w_dt"])
    proj = as3(proj)
    dt_raw = as3(dt_raw)

    r, k, v, kn, g, bonus, ld, a = _rw_prep(proj, p["mup"], p["mun"], p["w2"], p["a2"], p["g2"],
                                            p["w0"], p["a0"], p["k_k"], p["r_k"], p["hdown"], p["hup"])
    yw = _wkv(r, k, v, kn, ld, a, p["k_a"])

    xbc_act = _conv_silu(proj, p["conv_w"], p["conv_b"])
    ym = _ssd(xbc_act, dt_raw, p["dt_bias"], p["a_log"], p["hexp"])

    x1 = _merge(yw, bonus, g, ym, xbc_act, proj, x, p["hdown"], p["hup"], p["gn_g"], p["gn_b"], p["dskip"],
                p["norm_g"], p["w_br"], p["w_bm"], p["w_o"], p["ln1_g"], p["ln1_b"])

    kv = _mm(mem.reshape(n * N_MEM, dm), p["w_kv"]).reshape(n, N_MEM, 2 * dm)
    x2_ = _attention(x1, kv, p["w_q"], p["w_co"], p["ln2_g"], p["ln2_b"])
    out = _mlp(x2_.reshape(n * t, dm), p["w_up"], p["w_down"], p["ln3_g"], p["ln3_b"])
    return out.reshape(n, t, dm)


def kernel(x_prompt, x_sample, mem_prompt, mem_sample, w_in, rw_mu_prev, rw_mu_next, rw_w0, rw_w2, rw_a0, rw_a2, rw_g2, rw_k_k, rw_k_a, rw_r_k, rw_gn_g, rw_gn_b, m_conv_w, m_conv_b, m_dt_bias, m_a_log, m_d, m_norm_g, w_br, w_bm, w_o, ln1_g, ln1_b, w_q, w_kv, w_co, ln2_g, ln2_b, w_up, w_down, ln3_g, ln3_b):
    p = _prepare(w_in, rw_mu_prev, rw_mu_next, rw_w0, rw_w2, rw_a0, rw_a2, rw_g2, rw_k_k, rw_k_a, rw_r_k,
                 rw_gn_g, rw_gn_b, m_conv_w, m_conv_b, m_dt_bias, m_a_log, m_d, m_norm_g, w_br, w_bm, w_o,
                 ln1_g, ln1_b, w_q, w_kv, w_co, ln2_g, ln2_b, w_up, w_down, ln3_g, ln3_b)
    return (_encoder_layer(x_prompt, mem_prompt, p), _encoder_layer(x_sample, mem_sample, p))
```

```python
import functools
import math

import jax
import jax.numpy as jnp
from jax import lax
from jax.experimental import pallas as pl
from jax.experimental.pallas import tpu as pltpu

F32 = jnp.float32
BF16 = jnp.bfloat16

D_MODEL = 2048
RW_HEAD = 64
RW_WIDTH = D_MODEL // 2
RW_HEADS = RW_WIDTH // RW_HEAD
DECAY_LORA = 96
ICLR_LORA = 96
GATE_LORA = 256
LORA_PAD = 128
M_INNER = D_MODEL
M_HEADDIM = 64
M_HEADS = M_INNER // M_HEADDIM
M_STATE = 128
M_GROUPS = 8
CONV_W = 5
N_MEM = 256
X_HEADS = 4
X_HEAD_DIM = D_MODEL // X_HEADS
D_FF = 4 * D_MODEL
DEPTH = 1
ALPHA = (2.0 * DEPTH) ** 0.25
LN_EPS = 1e-5
GN_EPS = 64e-5

RW_SHIFT_COLS = 3 * RW_WIDTH + DECAY_LORA + ICLR_LORA + GATE_LORA
RW_PAD_COLS = 3 * RW_WIDTH + 2 * LORA_PAD + GATE_LORA
M_CONV_CH = M_INNER + 2 * M_GROUPS * M_STATE

WKV_CHUNK = 64
SSD_CHUNK = 128
HALO = 16
LANES = 128
VMEM_LIMIT = 56 * 1024 * 1024


def _cparams(sem):
    return pltpu.CompilerParams(dimension_semantics=sem, vmem_limit_bytes=VMEM_LIMIT)


def _resident(shape):
    nd = len(shape)
    return pl.BlockSpec(shape, lambda *_: (0,) * nd, pipeline_mode=pl.Buffered(1))


def _split3(a):
    a1 = a.astype(BF16)
    r1 = a - a1.astype(F32)
    a2 = r1.astype(BF16)
    a3 = (r1 - a2.astype(F32)).astype(BF16)
    return a1, a2, a3


def _dot(a, b):
    return jnp.dot(a, b, preferred_element_type=F32)


def _dot_nt(a, b):
    return lax.dot_general(a, b, (((1,), (1,)), ((), ())), preferred_element_type=F32)


def _sel_left(m01, a):
    a1, a2, a3 = _split3(a)
    return _dot(m01, a1) + _dot(m01, a2) + _dot(m01, a3)


def _sel_right(a, m01, pieces=2):
    a1, a2, a3 = _split3(a)
    out = _dot(a1, m01) + _dot(a2, m01)
    if pieces == 3:
        out = out + _dot(a3, m01)
    return out


def _head_sum(a, down, up):
    return _sel_right(_sel_right(a, down), up)


def _softplus(x):
    return jnp.maximum(x, 0.0) + jnp.log1p(jnp.exp(-jnp.abs(x)))


def _sigmoid(x):
    return 1.0 / (1.0 + jnp.exp(-x))


def _layer_norm(x, g, b):
    mu = jnp.mean(x, axis=-1, keepdims=True)
    xc = x - mu
    var = jnp.mean(xc * xc, axis=-1, keepdims=True)
    return xc * lax.rsqrt(var + LN_EPS) * g + b


def _mm_kernel(x_ref, w_ref, o_ref, xb_ref):
    @pl.when(pl.program_id(1) == 0)
    def _():
        xb_ref[...] = x_ref[...].astype(BF16)

    o_ref[...] = _dot(xb_ref[...], w_ref[...]).astype(o_ref.dtype)


def _mm(x, w, *, out_dtype=BF16, tm=1024, tn=512):
    m, k = x.shape
    n = w.shape[1]
    tm = min(tm, m)
    tn = min(tn, n)
    return pl.pallas_call(
        _mm_kernel,
        grid=(m // tm, n // tn),
        in_specs=[pl.BlockSpec((tm, k), lambda i, j: (i, 0)),
                  pl.BlockSpec((k, tn), lambda i, j: (0, j))],
        out_specs=pl.BlockSpec((tm, tn), lambda i, j: (i, j)),
        out_shape=jax.ShapeDtypeStruct((m, n), out_dtype),
        scratch_shapes=[pltpu.VMEM((tm, k), BF16)],
        compiler_params=_cparams(("parallel", "arbitrary")),
        name="matmul",
    )(x, w)


INPROJ_TN = 1024
INPROJ_GATE_CHUNK = 256
COL_RW = 0
COL_XBC = 4 * INPROJ_TN
COL_GATES = COL_XBC + M_CONV_CH
COL_Z = COL_GATES + 2 * D_MODEL
INPROJ_COLS = COL_Z + M_INNER


def _inproj_kernel(x_ref, w_ref, wdt_ref, o_ref, dt_ref, xb_ref):
    j = pl.program_id(1)

    @pl.when(j == 0)
    def _():
        xb = x_ref[...].astype(BF16)
        xb_ref[...] = xb
        dt_ref[...] = _dot(xb, wdt_ref[...])

    is_gate = jnp.logical_and(j >= COL_GATES // INPROJ_TN, j < COL_Z // INPROJ_TN)

    @pl.when(is_gate)
    def _():
        for c0 in range(0, INPROJ_TN, INPROJ_GATE_CHUNK):
            cl = slice(c0, c0 + INPROJ_GATE_CHUNK)
            o_ref[:, cl] = _sigmoid(_dot(xb_ref[...], w_ref[:, cl])).astype(o_ref.dtype)

    @pl.when(jnp.logical_not(is_gate))
    def _():
        o_ref[...] = _dot(xb_ref[...], w_ref[...]).astype(o_ref.dtype)


def _inproj(x, w_tiles, w_dt, tm=1024):
    m, k = x.shape
    tm = min(tm, m)
    tn = INPROJ_TN
    return pl.pallas_call(
        _inproj_kernel,
        grid=(m // tm, INPROJ_COLS // tn),
        in_specs=[pl.BlockSpec((tm, k), lambda i, j: (i, 0)),
                  pl.BlockSpec((k, tn), lambda i, j: (0, j)),
                  _resident(w_dt.shape)],
        out_specs=[pl.BlockSpec((tm, tn), lambda i, j: (i, j)),
                   pl.BlockSpec((tm, LANES), lambda i, j: (i, 0))],
        out_shape=[jax.ShapeDtypeStruct((m, INPROJ_COLS), BF16), jax.ShapeDtypeStruct((m, LANES), F32)],
        scratch_shapes=[pltpu.VMEM((tm, k), BF16)],
        compiler_params=_cparams(("parallel", "arbitrary")),
        name="in_projection",
    )(x, w_tiles, w_dt)


def _halo_specs(t, tt, width):
    per = tt // HALO
    last = t // HALO - 1

    def prev_map(b, i):
        return (b, jnp.maximum(i * per - 1, 0), 0)

    def next_map(b, i):
        return (b, jnp.minimum((i + 1) * per, last), 0)

    return (pl.BlockSpec((1, HALO, width), prev_map), pl.BlockSpec((1, HALO, width), next_map))


SHIFT_ROWS = 128
SHIFT_LANES = 256


def _shift_matrix(rb, off):
    row = lax.broadcasted_iota(jnp.int32, (rb, rb + 2 * HALO), 0)
    col = lax.broadcasted_iota(jnp.int32, (rb, rb + 2 * HALO), 1)
    return jnp.where(col - row == HALO + off, 1.0, 0.0).astype(BF16)


def _extended_block(x_ref, prev_rows, next_rows, r0, cl):
    tt = x_ref.shape[1]
    rb = SHIFT_ROWS
    above = prev_rows[:, cl] if r0 == 0 else x_ref[0, r0 - HALO:r0, cl]
    below = next_rows[:, cl] if r0 + rb == tt else x_ref[0, r0 + rb:r0 + rb + HALO, cl]
    xb = x_ref[0, r0:r0 + rb, cl]
    return xb, jnp.concatenate([above, xb, below], axis=0)


def _rw_prep_kernel(p_ref, pp_ref, pn_ref, mup_ref, mun_ref, w2_ref, a2_ref, g2_ref, w0_ref, a0_ref,
                    kk_ref, rk_ref, hdown_ref, hup_ref,
                    r_out, k_out, v_out, kn_out, g_out, bonus_out, ld_out, a_out, ps_ref):
    i = pl.program_id(1)
    nt = pl.num_programs(1)
    tt, c = p_ref.shape[1], p_ref.shape[2]
    prev_rows = jnp.where(i > 0, pp_ref[0], jnp.zeros_like(pp_ref[0]))
    next_rows = jnp.where(i < nt - 1, pn_ref[0], jnp.zeros_like(pn_ref[0]))
    s_prev = _shift_matrix(SHIFT_ROWS, -1)
    s_next = _shift_matrix(SHIFT_ROWS, 1)
    for r0 in range(0, tt, SHIFT_ROWS):
        for c0 in range(0, c, SHIFT_LANES):
            cl = slice(c0, c0 + SHIFT_LANES)
            pb, pe = _extended_block(p_ref, prev_rows, next_rows, r0, cl)
            p = pb.astype(F32)
            ps_ref[r0:r0 + SHIFT_ROWS, cl] = (p + mup_ref[:, cl] * (_dot(s_prev, pe) - p)
                                               + mun_ref[:, cl] * (_dot(s_next, pe) - p))
    w = RW_WIDTH
    r = ps_ref[:, 0:w]
    k = ps_ref[:, w:2 * w]
    v = ps_ref[:, 2 * w:3 * w]
    dw = ps_ref[:, 3 * w:3 * w + LORA_PAD]
    da = ps_ref[:, 3 * w + LORA_PAD:3 * w + 2 * LORA_PAD]
    dg = ps_ref[:, 3 * w + 2 * LORA_PAD:]
    hw = _dot(jnp.tanh(dw).astype(BF16), w2_ref[...])
    ha = _dot(da.astype(BF16), a2_ref[...])
    g = _dot(_sigmoid(dg).astype(BF16), g2_ref[...])
    for d in range(2):
        ld_out[d, 0] = -math.exp(-0.5) * _sigmoid(w0_ref[d:d + 1, :] + hw)
        a_out[d, 0] = _sigmoid(a0_ref[d:d + 1, :] + ha)
    hdown = hdown_ref[...]
    hup = hup_ref[...]
    kkr = k * kk_ref[...]
    ss = _head_sum(kkr * kkr, hdown, hup)
    kn = kkr * lax.rsqrt(jnp.maximum(ss, 1e-24))
    rk = _head_sum(r * k * rk_ref[...], hdown, hup)
    r_out[0] = r.astype(r_out.dtype)
    k_out[0] = k.astype(k_out.dtype)
    v_out[0] = v.astype(v_out.dtype)
    kn_out[0] = kn.astype(kn_out.dtype)
    g_out[0] = g.astype(g_out.dtype)
    bonus_out[0] = (rk * v).astype(bonus_out.dtype)


def _rw_prep(p, mup, mun, w2, a2, g2, w0, a0, k_k, r_k, hdown, hup, tt=256):
    n, t, _ = p.shape
    c = RW_PAD_COLS
    tt = min(tt, t)
    w = RW_WIDTH
    tile = lambda width: pl.BlockSpec((1, tt, width), lambda b, i: (b, i, 0))
    dir_tile = pl.BlockSpec((2, 1, tt, w), lambda b, i: (0, b, i, 0))
    hp, hn = _halo_specs(t, tt, c)
    bf = jax.ShapeDtypeStruct((n, t, w), BF16)
    dd = jax.ShapeDtypeStruct((2, n, t, w), F32)
    return pl.pallas_call(
        _rw_prep_kernel,
        grid=(n, t // tt),
        in_specs=[tile(c), hp, hn, _resident(mup.shape), _resident(mun.shape), _resident(w2.shape),
                  _resident(a2.shape), _resident(g2.shape), _resident(w0.shape), _resident(a0.shape),
                  _resident(k_k.shape), _resident(r_k.shape), _resident(hdown.shape), _resident(hup.shape)],
        out_specs=[tile(w)] * 6 + [dir_tile, dir_tile],
        out_shape=[bf] * 6 + [dd, dd],
        scratch_shapes=[pltpu.VMEM((tt, c), F32)],
        compiler_params=_cparams(("parallel", "parallel")),
        name="rwkv_prep",
    )(p, p, p, mup, mun, w2, a2, g2, w0, a0, k_k, r_k, hdown, hup)


def _wkv_kernel(r_ref, k_ref, v_ref, kn_ref, ld_ref, a_ref, ka_ref, y_ref, h_ref, *, nchunk):
    d = pl.program_id(1)
    c = pl.program_id(2)

    @pl.when(c == 0)
    def _():
        h_ref[...] = jnp.zeros_like(h_ref)

    cs = WKV_CHUNK
    npair = RW_HEADS // 2
    sgn = 1 - 2 * d
    row = lax.broadcasted_iota(jnp.int32, (cs, 2 * cs), 0)
    col = lax.broadcasted_iota(jnp.int32, (cs, 2 * cs), 1) & (cs - 1)
    lag2 = (row - col) * sgn
    strict2 = lag2 > 0
    incl2 = lag2 >= 0
    m_incl = jnp.where(incl2[:, :cs], 1.0, 0.0).astype(BF16)
    eye = jnp.where(lax.broadcasted_iota(jnp.int32, (cs, cs), 0)
                    == lax.broadcasted_iota(jnp.int32, (cs, cs), 1), 1.0, 0.0)
    lane = lax.broadcasted_iota(jnp.int32, (1, LANES), 1)
    left = lane < RW_HEAD
    right = lane >= RW_HEAD
    left2 = (lax.broadcasted_iota(jnp.int32, (cs, 2 * LANES), 1) & (LANES - 1)) < RW_HEAD
    blockdiag = (lax.broadcasted_iota(jnp.int32, (LANES, LANES), 0) // RW_HEAD
                 == lax.broadcasted_iota(jnp.int32, (LANES, LANES), 1) // RW_HEAD)
    eye_l = jnp.where(lax.broadcasted_iota(jnp.int32, (LANES, LANES), 0)
                      == lax.broadcasted_iota(jnp.int32, (LANES, LANES), 1), 1.0, 0.0).astype(BF16)
    zeros_cv = jnp.zeros((cs, LANES), BF16)
    zeros_cf = jnp.zeros((cs, LANES), F32)
    ka = ka_ref[...]

    offs, prep = [], []
    for j in range(nchunk):
        off = pl.multiple_of((j + d * (nchunk - 1 - 2 * j)) * cs, cs)
        offs.append(off)
        ld = ld_ref[0, 0, pl.ds(off, cs), :]
        cum = _sel_left(m_incl, ld)
        tot = jnp.sum(ld, axis=0, keepdims=True)
        r = r_ref[0, pl.ds(off, cs), :].astype(F32)
        k = k_ref[0, pl.ds(off, cs), :].astype(F32)
        kn = kn_ref[0, pl.ds(off, cs), :].astype(F32)
        a = a_ref[0, 0, pl.ds(off, cs), :]
        kdir = k * (1.0 + (a - 1.0) * ka)
        b = kn * a
        e_ncum = jnp.exp(-cum)
        wc = jnp.exp(tot)
        ktf = kdir * e_ncum
        btf = b * e_ncum
        prep.append(dict(
            rt=(r * jnp.exp(cum)).astype(BF16), at=(-kn * jnp.exp(cum - ld)).astype(BF16),
            kt=ktf.astype(BF16), bt=btf.astype(BF16),
            kh=(ktf * wc).astype(BF16), bh=(btf * wc).astype(BF16),
            v=v_ref[0, pl.ds(off, cs), :], wc=wc))

    units = [(j, p) for j in range(nchunk) for p in range(npair)]
    heads = [(j, p, hh) for (j, p) in units for hh in range(2)]
    sl = lambda p: slice(p * LANES, (p + 1) * LANES)

    g_up, g_low, a_ab = {}, {}, {}
    for (j, p) in units:
        q = prep[j]
        ar = jnp.concatenate([q["at"][:, sl(p)], q["rt"][:, sl(p)]], axis=0)
        bk = jnp.concatenate([q["bt"][:, sl(p)], q["kt"][:, sl(p)]], axis=0)
        for hh in range(2):
            arm = jnp.where(left if hh == 0 else right, ar, jnp.zeros_like(ar))
            gm = _dot_nt(arm, bk)
            gu = jnp.where(strict2, gm[:cs], 0.0)
            a_ab[j, p, hh] = gu[:, :cs]
            g_up[j, p, hh] = gu.astype(BF16)
            g_low[j, p, hh] = jnp.where(incl2, gm[cs:], 0.0).astype(BF16)

    tinv = {h: a_ab[h] + eye for h in heads}
    pw = dict(a_ab)
    for _ in range(int(math.log2(cs)) - 1):
        for h in heads:
            pb = pw[h].astype(BF16)
            pw[h] = _dot(pb, pb)
        for h in heads:
            tinv[h] = tinv[h] + _dot(tinv[h].astype(BF16), pw[h].astype(BF16))

    w1 = {}
    for (j, p) in units:
        zv = jnp.concatenate([zeros_cv, prep[j]["v"][:, sl(p)]], axis=0)
        for hh in range(2):
            w1[j, p, hh] = _dot(g_up[j, p, hh], zv)
    tx = {}
    for (j, p, hh) in heads:
        rhs = jnp.concatenate([w1[j, p, hh].astype(BF16), prep[j]["at"][:, sl(p)]], axis=1)
        tx[j, p, hh] = _dot(tinv[j, p, hh].astype(BF16), rhs)
    lmat = {}
    for (j, p) in units:
        va = jnp.where(left2, tx[j, p, 0], tx[j, p, 1])
        vrow = jnp.concatenate([prep[j]["v"][:, sl(p)].astype(F32), zeros_cf], axis=1)
        lmat[j, p] = jnp.concatenate([va, vrow], axis=0).astype(BF16)

    z = {h: _dot(g_low[h], lmat[h[0], h[1]]) for h in heads}
    lv_t = {u: _dot_nt(eye_l, lmat[u][:, :LANES]).astype(BF16) for u in units}
    bh_t = {(j, p): _dot_nt(eye_l, prep[j]["bh"][:, sl(p)]).astype(BF16) for (j, p) in units}
    y0, m1, m2, n2t = {}, {}, {}, {}
    for (j, p) in units:
        q = prep[j]
        ym = jnp.where(left2, z[j, p, 0], z[j, p, 1])
        y0[j, p] = ym[:, :LANES]
        m1[j, p] = (q["rt"][:, sl(p)].astype(F32) + ym[:, LANES:]).astype(BF16)
        bkh = jnp.concatenate([q["bh"][:, sl(p)], q["kh"][:, sl(p)]], axis=0)
        n2t[j, p] = jnp.where(blockdiag, _dot(lv_t[j, p], bkh), 0.0)
    for (j, p) in units:
        m2[j, p] = jnp.where(blockdiag, _dot(bh_t[j, p], lmat[j, p][:cs, LANES:]), 0.0).astype(BF16)

    hts = [h_ref[p] for p in range(npair)]
    for j in range(nchunk):
        for p in range(npair):
            htb = hts[p].astype(BF16)
            y = y0[j, p] + _dot_nt(m1[j, p], htb)
            y_ref[0, 0, pl.ds(offs[j], cs), sl(p)] = y.astype(y_ref.dtype)
            hts[p] = hts[p] * prep[j]["wc"][:, sl(p)] + _dot_nt(htb, m2[j, p]) + n2t[j, p]
    for p in range(npair):
        h_ref[p] = hts[p]


def _wkv(r, k, v, kn, ld, a, k_a, nchunk=4):
    n, t, w = r.shape
    tb = WKV_CHUNK * nchunk
    nb = t // tb

    def tmap(b, d, c):
        return (b, c + d * (nb - 1 - 2 * c), 0)

    def dmap(b, d, c):
        return (d, b, c + d * (nb - 1 - 2 * c), 0)

    tile = pl.BlockSpec((1, tb, w), tmap)
    dtile = pl.BlockSpec((1, 1, tb, w), dmap)
    return pl.pallas_call(
        functools.partial(_wkv_kernel, nchunk=nchunk),
        grid=(n, 2, nb),
        in_specs=[tile, tile, tile, tile, dtile, dtile, _resident(k_a.shape)],
        out_specs=dtile,
        out_shape=jax.ShapeDtypeStruct((2, n, t, w), BF16),
        scratch_shapes=[pltpu.VMEM((RW_HEADS // 2, LANES, LANES), F32)],
        compiler_params=_cparams(("parallel", "arbitrary", "arbitrary")),
        name="wkv7_scan",
    )(r, k, v, kn, ld, a, k_a)


def _conv_kernel(x_ref, xp_ref, xn_ref, w_ref, b_ref, o_ref):
    i = pl.program_id(1)
    nt = pl.num_programs(1)
    tt, tc = x_ref.shape[1], x_ref.shape[2]
    half = CONV_W // 2
    rb = SHIFT_ROWS
    shifts = {j: _shift_matrix(rb, j - half) for j in range(CONV_W) if j != half}
    prev_rows = jnp.where(i > 0, xp_ref[0], jnp.zeros_like(xp_ref[0]))
    next_rows = jnp.where(i < nt - 1, xn_ref[0], jnp.zeros_like(xn_ref[0]))
    for r0 in range(0, tt, rb):
        for c0 in range(0, tc, SHIFT_LANES):
            cl = slice(c0, c0 + SHIFT_LANES)
            xb, xe = _extended_block(x_ref, prev_rows, next_rows, r0, cl)
            acc = xb.astype(F32) * w_ref[half:half + 1, cl] + b_ref[:, cl]
            for j, sh in shifts.items():
                acc = acc + _dot(sh, xe) * w_ref[j:j + 1, cl]
            o_ref[0, r0:r0 + rb, cl] = (acc * _sigmoid(acc)).astype(o_ref.dtype)


def _conv_silu(proj, w, b, tt=256, tc=1024):
    n, t, _ = proj.shape
    c = M_CONV_CH
    j0 = COL_XBC // tc
    tt = min(tt, t)
    per = tt // HALO
    nrow = t // HALO
    return pl.pallas_call(
        _conv_kernel,
        grid=(n, t // tt, c // tc),
        in_specs=[pl.BlockSpec((1, tt, tc), lambda b_, i, j: (b_, i, j0 + j)),
                  pl.BlockSpec((1, HALO, tc), lambda b_, i, j: (b_, jnp.maximum(i * per - 1, 0), j0 + j)),
                  pl.BlockSpec((1, HALO, tc), lambda b_, i, j: (b_, jnp.minimum((i + 1) * per, nrow - 1), j0 + j)),
                  pl.BlockSpec((CONV_W, tc), lambda b_, i, j: (0, j)),
                  pl.BlockSpec((1, tc), lambda b_, i, j: (0, j))],
        out_specs=pl.BlockSpec((1, tt, tc), lambda b_, i, j: (b_, i, j)),
        out_shape=jax.ShapeDtypeStruct((n, t, c), BF16),
        compiler_params=_cparams(("parallel", "parallel", "parallel")),
        name="mamba_conv",
    )(proj, proj, proj, w, b)


def _ssd_kernel(xbc_ref, dt_ref, dtb_ref, alog_ref, hexp_ref, y_ref, h_ref):
    d = pl.program_id(1)
    c = pl.program_id(2)

    @pl.when(c == 0)
    def _():
        h_ref[...] = jnp.zeros_like(h_ref)

    ln = SSD_CHUNK
    row = lax.broadcasted_iota(jnp.int32, (ln, ln), 0)
    col = lax.broadcasted_iota(jnp.int32, (ln, ln), 1)
    incl = (row - col) * (1 - 2 * d) >= 0
    m_incl = jnp.where(incl, 1.0, 0.0).astype(BF16)

    dt = _softplus(dt_ref[0] + dtb_ref[0])
    da = dt * (-jnp.exp(alog_ref[0]))
    cum = _sel_left(m_incl, da)
    tot = jnp.sum(da, axis=0, keepdims=True)
    cum_t = cum.T
    hexp = hexp_ref[...]
    dt_x = _dot(dt.astype(BF16), hexp)
    dst_x = _dot((jnp.exp(tot - cum) * dt).astype(BF16), hexp)
    ecum_x = _dot(jnp.exp(cum).astype(BF16), hexp)
    cdec = jnp.exp(jnp.broadcast_to(tot, (8, LANES)))
    cdec_x = _sel_right(cdec, hexp, pieces=3)[0:1]

    x = xbc_ref[0, :, 0:M_INNER].astype(F32)
    xdt = (x * dt_x).astype(BF16)
    xdst = (x * dst_x).astype(BF16)

    lane = lax.broadcasted_iota(jnp.int32, (1, LANES), 1)
    left = lane < M_HEADDIM
    eye_l = jnp.where(row == col, 1.0, 0.0).astype(BF16)
    gw = M_INNER // M_GROUPS
    hpg = M_HEADS // M_GROUPS
    b_off = M_INNER
    c_off = M_INNER + M_GROUPS * M_STATE
    groups = range(M_GROUPS)
    gsl = lambda g: slice(g * gw, (g + 1) * gw)
    bg = {g: xbc_ref[0, :, b_off + g * M_STATE:b_off + (g + 1) * M_STATE] for g in groups}
    cg = {g: xbc_ref[0, :, c_off + g * M_STATE:c_off + (g + 1) * M_STATE] for g in groups}
    cb = {g: _dot_nt(cg[g], bg[g]) for g in groups}
    hg = {g: h_ref[g] for g in groups}
    y_in = {g: _dot(cg[g], hg[g].astype(BF16)) * ecum_x[:, gsl(g)] for g in groups}
    bg_t = {g: _dot_nt(eye_l, bg[g]).astype(BF16) for g in groups}
    for g in groups:
        h_ref[g] = hg[g] * cdec_x[:, gsl(g)] + _dot(bg_t[g], xdst[:, gsl(g)])
    for g in groups:
        for pr in range(hpg // 2):
            lo = g * gw + pr * LANES
            ys = []
            for hh in range(2):
                e = g * hpg + 2 * pr + hh
                seg = cum[:, e:e + 1] - cum_t[e:e + 1, :]
                decay = jnp.exp(jnp.where(incl, seg, -1e30))
                ys.append(_dot((cb[g] * decay).astype(BF16), xdt[:, lo:lo + LANES]))
            y = jnp.where(left, ys[0], ys[1]) + y_in[g][:, pr * LANES:(pr + 1) * LANES]
            y_ref[0, 0, :, lo:lo + LANES] = y.astype(y_ref.dtype)


def _ssd(xbc_act, dt_raw, dt_bias, a_log, hexp):
    n, t, c = xbc_act.shape
    ln = SSD_CHUNK
    nc = t // ln

    def tmap(b, d, ci):
        return (b, ci + d * (nc - 1 - 2 * ci), 0)

    def dmap(b, d, ci):
        return (d, b, ci + d * (nc - 1 - 2 * ci), 0)

    par = pl.BlockSpec((1, 1, LANES), lambda b, d, ci: (d, 0, 0))
    return pl.pallas_call(
        _ssd_kernel,
        grid=(n, 2, nc),
        in_specs=[pl.BlockSpec((1, ln, c), tmap), pl.BlockSpec((1, ln, LANES), tmap), par, par,
                  _resident(hexp.shape)],
        out_specs=pl.BlockSpec((1, 1, ln, M_INNER), dmap),
        out_shape=jax.ShapeDtypeStruct((2, n, t, M_INNER), BF16),
        scratch_shapes=[pltpu.VMEM((M_GROUPS, M_STATE, M_INNER // M_GROUPS), F32)],
        compiler_params=_cparams(("parallel", "arbitrary", "arbitrary")),
        name="ssd_scan",
    )(xbc_act, dt_raw, dt_bias, a_log, hexp)


def _merge_kernel(yw_ref, bonus_ref, g_ref, ym_ref, xs_ref, z_ref, gates_ref, x_ref,
                  hdown_ref, hup_ref, gng_ref, gnb_ref, dskip_ref, ng_ref, wbr_ref, wbm_ref, wo_ref,
                  lng_ref, lnb_ref, o_ref):
    y = yw_ref[0, 0].astype(F32) + yw_ref[1, 0].astype(F32)
    hdown = hdown_ref[...]
    hup = hup_ref[...]
    mu = _head_sum(y, hdown, hup) * (1.0 / RW_HEAD)
    yc = y - mu
    var = _head_sum(yc * yc, hdown, hup) * (1.0 / RW_HEAD)
    yn = yc * lax.rsqrt(var + GN_EPS) * gng_ref[...] + gnb_ref[...]
    rw = (yn + bonus_ref[0].astype(F32)) * g_ref[0].astype(F32)
    z = z_ref[0].astype(F32)
    ym = ym_ref[0, 0].astype(F32) + ym_ref[1, 0].astype(F32) + dskip_ref[...] * xs_ref[0].astype(F32)
    ym = ym * (z * _sigmoid(z))
    gw = M_INNER // M_GROUPS
    parts = []
    for gi in range(M_GROUPS):
        yg = ym[:, gi * gw:(gi + 1) * gw]
        ms = jnp.mean(yg * yg, axis=-1, keepdims=True)
        parts.append(yg * lax.rsqrt(ms + LN_EPS))
    mo = jnp.concatenate(parts, axis=1) * ng_ref[...]
    u_rw = _dot(rw.astype(BF16), wbr_ref[...])
    u_m = _dot(mo.astype(BF16), wbm_ref[...])
    gates = gates_ref[0]
    mixed = gates[:, :D_MODEL].astype(F32) * u_rw + gates[:, D_MODEL:].astype(F32) * u_m
    mix = _dot(mixed.astype(BF16), wo_ref[...])
    o_ref[0] = _layer_norm(ALPHA * x_ref[0] + mix, lng_ref[...], lnb_ref[...])


def _merge(yw, bonus, g, ym, xbc_act, proj, x, hdown, hup, gn_g, gn_b, dskip, norm_g, w_br, w_bm, w_o,
           ln_g, ln_b, tm=256):
    n, t, _ = x.shape
    tm = min(tm, t)
    tile = lambda width: pl.BlockSpec((1, tm, width), lambda b, i: (b, i, 0))
    dtile = lambda width: pl.BlockSpec((2, 1, tm, width), lambda b, i: (0, b, i, 0))
    res = [hdown, hup, gn_g, gn_b, dskip, norm_g, w_br, w_bm, w_o, ln_g, ln_b]
    return pl.pallas_call(
        _merge_kernel,
        grid=(n, t // tm),
        in_specs=[dtile(RW_WIDTH), tile(RW_WIDTH), tile(RW_WIDTH), dtile(M_INNER), tile(M_INNER),
                  pl.BlockSpec((1, tm, M_INNER), lambda b, i: (b, i, COL_Z // M_INNER)),
                  pl.BlockSpec((1, tm, 2 * D_MODEL), lambda b, i: (b, i, COL_GATES // (2 * D_MODEL))),
                  tile(D_MODEL)] + [_resident(a.shape) for a in res],
        out_specs=tile(D_MODEL),
        out_shape=jax.ShapeDtypeStruct(x.shape, F32),
        compiler_params=_cparams(("parallel", "parallel")),
        name="merge_ln1",
    )(yw, bonus, g, ym, xbc_act, proj, proj, x, *res)


def _attn_kernel(x_ref, kv_ref, wq_ref, wco_ref, lng_ref, lnb_ref, o_ref):
    x = x_ref[0]
    q = _dot(x.astype(BF16), wq_ref[...]).astype(BF16)
    scale = 1.0 / math.sqrt(X_HEAD_DIM)
    hs = range(X_HEADS)
    sl = lambda h: slice(h * X_HEAD_DIM, (h + 1) * X_HEAD_DIM)
    s = [_dot_nt(q[:, sl(h)], kv_ref[0, :, sl(h)]) * scale for h in hs]
    pr = []
    for h in hs:
        e = jnp.exp(s[h] - jnp.max(s[h], axis=-1, keepdims=True))
        pr.append((e / jnp.sum(e, axis=-1, keepdims=True)).astype(BF16))
    o = [_dot(pr[h], kv_ref[0, :, D_MODEL + h * X_HEAD_DIM:D_MODEL + (h + 1) * X_HEAD_DIM]).astype(BF16) for h in hs]
    out = _dot(jnp.concatenate(o, axis=1), wco_ref[...])
    o_ref[0] = _layer_norm(ALPHA * x + out, lng_ref[...], lnb_ref[...])


def _attention(x, kv, w_q, w_co, ln_g, ln_b, tm=512):
    n, t, _ = x.shape
    tm = min(tm, t)
    tile = pl.BlockSpec((1, tm, D_MODEL), lambda b, i: (b, i, 0))
    return pl.pallas_call(
        _attn_kernel,
        grid=(n, t // tm),
        in_specs=[tile, pl.BlockSpec((1, N_MEM, 2 * D_MODEL), lambda b, i: (b, 0, 0)),
                  _resident(w_q.shape), _resident(w_co.shape), _resident(ln_g.shape), _resident(ln_b.shape)],
        out_specs=tile,
        out_shape=jax.ShapeDtypeStruct(x.shape, F32),
        compiler_params=_cparams(("parallel", "parallel")),
        name="mem_attention_ln2",
    )(x, kv, w_q, w_co, ln_g, ln_b)


def _mlp_kernel(x_ref, wu_ref, wd_ref, lng_ref, lnb_ref, o_ref, acc_ref, xb_ref):
    f = pl.program_id(1)

    @pl.when(f == 0)
    def _():
        xb_ref[...] = x_ref[...].astype(BF16)
        acc_ref[...] = jnp.zeros_like(acc_ref)

    h = jnp.maximum(_dot(xb_ref[...], wu_ref[...]), 0.0)
    acc_ref[...] += _dot((h * h).astype(BF16), wd_ref[...])

    @pl.when(f == pl.num_programs(1) - 1)
    def _():
        o_ref[...] = _layer_norm(ALPHA * x_ref[...] + acc_ref[...], lng_ref[...], lnb_ref[...])


def _mlp(x, w_up, w_down, ln_g, ln_b, tm=512, tf=1024):
    m, dm = x.shape
    tm = min(tm, m)
    return pl.pallas_call(
        _mlp_kernel,
        grid=(m // tm, D_FF // tf),
        in_specs=[pl.BlockSpec((tm, dm), lambda i, f: (i, 0)),
                  pl.BlockSpec((dm, tf), lambda i, f: (0, f)),
                  pl.BlockSpec((tf, dm), lambda i, f: (f, 0)),
                  _resident(ln_g.shape), _resident(ln_b.shape)],
        out_specs=pl.BlockSpec((tm, dm), lambda i, f: (i, 0)),
        out_shape=jax.ShapeDtypeStruct((m, dm), F32),
        scratch_shapes=[pltpu.VMEM((tm, dm), F32), pltpu.VMEM((tm, dm), BF16)],
        compiler_params=_cparams(("parallel", "arbitrary")),
        name="mlp_ln3",
    )(x, w_up, w_down, ln_g, ln_b)


def _pad_cols(a, width):
    return jnp.pad(a, [(0, 0)] * (a.ndim - 1) + [(0, width - a.shape[-1])])


def _rw_cols(a):
    w3 = 3 * RW_WIDTH
    return jnp.concatenate([a[..., :w3],
                            _pad_cols(a[..., w3:w3 + DECAY_LORA], LORA_PAD),
                            _pad_cols(a[..., w3 + DECAY_LORA:w3 + DECAY_LORA + ICLR_LORA], LORA_PAD),
                            a[..., w3 + DECAY_LORA + ICLR_LORA:RW_SHIFT_COLS]], axis=-1)


def _prepare(w_in, rw_mu_prev, rw_mu_next, rw_w0, rw_w2, rw_a0, rw_a2, rw_g2, rw_k_k, rw_k_a, rw_r_k,
             rw_gn_g, rw_gn_b, m_conv_w, m_conv_b, m_dt_bias, m_a_log, m_d, m_norm_g, w_br, w_bm, w_o,
             ln1_g, ln1_b, w_q, w_kv, w_co, ln2_g, ln2_b, w_up, w_down, ln3_g, ln3_b):
    l = 0
    w = w_in[l]
    c0 = RW_SHIFT_COLS
    c1 = c0 + M_INNER
    c2 = c1 + M_CONV_CH
    c3 = c2 + M_HEADS
    row = lambda a: a.reshape(1, -1).astype(F32)
    head = jnp.arange(RW_WIDTH) // RW_HEAD
    lane_head = jnp.arange(M_INNER) // M_HEADDIM
    return dict(
        w_tiles=jnp.concatenate([_pad_cols(_rw_cols(w[:, :c0]), COL_XBC), w[:, c1:c2], w[:, c3:], w[:, c0:c1]],
                                axis=1).astype(BF16),
        w_dt=_pad_cols(w[:, c2:c3], LANES).astype(BF16),
        mup=row(_rw_cols(rw_mu_prev[l])), mun=row(_rw_cols(rw_mu_next[l])),
        w0=rw_w0[l].astype(F32), a0=rw_a0[l].astype(F32),
        w2=jnp.pad(rw_w2[l], ((0, LORA_PAD - DECAY_LORA), (0, 0))).astype(BF16),
        a2=jnp.pad(rw_a2[l], ((0, LORA_PAD - ICLR_LORA), (0, 0))).astype(BF16),
        g2=rw_g2[l].astype(BF16),
        k_k=row(rw_k_k[l]), k_a=row(rw_k_a[l]), r_k=row(rw_r_k[l]),
        gn_g=row(rw_gn_g[l]), gn_b=row(rw_gn_b[l]),
        hdown=(head[:, None] == jnp.arange(LANES)[None, :]).astype(BF16),
        hup=(jnp.arange(LANES)[:, None] == head[None, :]).astype(BF16),
        conv_w=m_conv_w[l].astype(F32), conv_b=row(m_conv_b[l]),
        dt_bias=_pad_cols(m_dt_bias[l], LANES).reshape(2, 1, LANES).astype(F32),
        a_log=_pad_cols(m_a_log[l], LANES).reshape(2, 1, LANES).astype(F32),
        hexp=(jnp.arange(LANES)[:, None] == lane_head[None, :]).astype(BF16),
        dskip=row(jnp.repeat(m_d[l], M_HEADDIM)), norm_g=row(m_norm_g[l]),
        w_br=w_br[l].astype(BF16), w_bm=w_bm[l].astype(BF16), w_o=w_o[l].astype(BF16),
        ln1_g=row(ln1_g[l]), ln1_b=row(ln1_b[l]),
        w_q=w_q[l].astype(BF16), w_kv=w_kv[l].astype(BF16), w_co=w_co[l].astype(BF16),
        ln2_g=row(ln2_g[l]), ln2_b=row(ln2_b[l]),
        w_up=w_up[l].astype(BF16), w_down=w_down[l].astype(BF16),
        ln3_g=row(ln3_g[l]), ln3_b=row(ln3_b[l]),
    )


def _encoder_layer(x, mem, p):
    n, t, dm = x.shape
    x2 = x.reshape(n * t, dm)
    as3 = lambda a: a.reshape(n, t, a.shape[-1])
    proj, dt_raw = _inproj(x2, p["w_tiles"], p["w_dt"])
    proj = as3(proj)
    dt_raw = as3(dt_raw)

    r, k, v, kn, g, bonus, ld, a = _rw_prep(proj, p["mup"], p["mun"], p["w2"], p["a2"], p["g2"],
                                            p["w0"], p["a0"], p["k_k"], p["r_k"], p["hdown"], p["hup"])
    yw = _wkv(r, k, v, kn, ld, a, p["k_a"])

    xbc_act = _conv_silu(proj, p["conv_w"], p["conv_b"])
    ym = _ssd(xbc_act, dt_raw, p["dt_bias"], p["a_log"], p["hexp"])

    x1 = _merge(yw, bonus, g, ym, xbc_act, proj, x, p["hdown"], p["hup"], p["gn_g"], p["gn_b"], p["dskip"],
                p["norm_g"], p["w_br"], p["w_bm"], p["w_o"], p["ln1_g"], p["ln1_b"])

    kv = _mm(mem.reshape(n * N_MEM, dm), p["w_kv"]).reshape(n, N_MEM, 2 * dm)
    x2_ = _attention(x1, kv, p["w_q"], p["w_co"], p["ln2_g"], p["ln2_b"])
    out = _mlp(x2_.reshape(n * t, dm), p["w_up"], p["w_down"], p["ln3_g"], p["ln3_b"])
    return out.reshape(n, t, dm)


def kernel(x_prompt, x_sample, mem_prompt, mem_sample, w_in, rw_mu_prev, rw_mu_next, rw_w0, rw_w2, rw_a0, rw_a2, rw_g2, rw_k_k, rw_k_a, rw_r_k, rw_gn_g, rw_gn_b, m_conv_w, m_conv_b, m_dt_bias, m_a_log, m_d, m_norm_g, w_br, w_bm, w_o, ln1_g, ln1_b, w_q, w_kv, w_co, ln2_g, ln2_b, w_up, w_down, ln3_g, ln3_b):
    p = _prepare(w_in, rw_mu_prev, rw_mu_next, rw_w0, rw_w2, rw_a0, rw_a2, rw_g2, rw_k_k, rw_k_a, rw_r_k,
                 rw_gn_g, rw_gn_b, m_conv_w, m_conv_b, m_dt_bias, m_a_log, m_d, m_norm_g, w_br, w_bm, w_o,
                 ln1_g, ln1_b, w_q, w_kv, w_co, ln2_g, ln2_b, w_up, w_down, ln3_g, ln3_b)
    return (_encoder_layer(x_prompt, mem_prompt, p), _encoder_layer(x_sample, mem_sample, p))
```
